```python
import math
import jax
import jax.numpy as jnp
from jax import lax
import numpy as np

D_MODEL = 1024
BATCH = 16
SEQ = 2048
DEPTH = 4

GRID_W = 64
CTX_LEN = 256
N_MIXERS = 2
N_ATTN_LAYERS = (DEPTH + 1) // 2
N_MLSTM_LAYERS = DEPTH // 2
N_MOD = 9
D_FF = 2816
RMS_EPS = 1e-6
ATTN_HEADS = 8
ATTN_HEAD_DIM = D_MODEL // (2 * ATTN_HEADS)
ATTN_V_DIM = 2 * ATTN_HEAD_DIM
ROPE_THETA = 10000.0
Q_BLOCK = 128
SUBLN_EPS = 1e-5
MLSTM_INNER = 2 * D_MODEL
MLSTM_HEADS = 4
MLSTM_HEAD_DIM = MLSTM_INNER // MLSTM_HEADS
QKV_BLOCK = 4
CONV_W = 5
CHUNK = 128
HEAD_LN_EPS = 1e-5

kernel_name = 'hybrid_diffattn_mlstm_macaron_dit'


def rmsnorm(x, w, eps=RMS_EPS):
    xf = x.astype(jnp.float32)
    y = xf * lax.rsqrt(jnp.mean(xf * xf, axis=-1, keepdims=True) + eps)
    return (y * w.astype(jnp.float32)).astype(x.dtype)


def adaln_input(h, norm_w, mod, j):
    return rmsnorm(h, norm_w) * (1 + mod[:, :, 3 * j + 1]) + mod[:, :, 3 * j]


def swiglu(u, w_in, w_out):
    g, up = jnp.split(u @ w_in, 2, axis=-1)
    return (jax.nn.silu(g) * up) @ w_out


def half_ffn(h, norm_w, mod, j, w_in, w_out):
    return h + 0.5 * mod[:, :, 3 * j + 2] * swiglu(adaln_input(h, norm_w, mod, j), w_in, w_out)


def axial_rope(x):
    b_, n_tok, nh, two, dh = x.shape
    rows = n_tok // GRID_W
    row_pos = jnp.broadcast_to(jnp.arange(rows, dtype=jnp.float32)[:, None], (rows, GRID_W)).reshape(-1)
    col_pos = jnp.broadcast_to(jnp.arange(GRID_W, dtype=jnp.float32)[None, :], (rows, GRID_W)).reshape(-1)
    half = dh // 2
    inv_freq = 1.0 / (ROPE_THETA ** (jnp.arange(0, half, 2, dtype=jnp.float32) / half))
    ang = jnp.stack([row_pos[:, None] * inv_freq, col_pos[:, None] * inv_freq], axis=1)
    cos = jnp.cos(ang)[None, :, None, None].astype(x.dtype)
    sin = jnp.sin(ang)[None, :, None, None].astype(x.dtype)
    xr = x.reshape(b_, n_tok, nh, two, 2, 2, dh // 4)
    x1, x2 = xr[..., 0, :], xr[..., 1, :]
    out = jnp.stack([x1 * cos - x2 * sin, x2 * cos + x1 * sin], axis=-2)
    return out.reshape(x.shape)


def diff_attend(q, k, v, lam):
    s = jnp.einsum('bhcqd,bhckd->bhcqk', q, k).astype(jnp.float32) * (ATTN_HEAD_DIM ** -0.5)
    p = jax.nn.softmax(s, axis=-1)
    a = p[:, :, 0] - lam * p[:, :, 1]
    return jnp.einsum('bhqk,bhkv->bhqv', a.astype(v.dtype), v)


def diff_attention_mixer(u_lat, u_ctx, w_qkv, w_o, lam_vecs, subln_w, layer_idx, need_ctx):
    b_, n_tok, d = u_lat.shape
    lambda_init = 0.8 - 0.6 * math.exp(-0.3 * layer_idx)
    lv = lam_vecs.astype(jnp.float32)
    lam = jnp.exp(jnp.sum(lv[0] * lv[1])) - jnp.exp(jnp.sum(lv[2] * lv[3])) + lambda_init

    def project(u):
        n = u.shape[1]
        q, k, v = jnp.split(u @ w_qkv, 3, axis=-1)
        return (q.reshape(b_, n, ATTN_HEADS, 2, ATTN_HEAD_DIM),
                k.reshape(b_, n, ATTN_HEADS, 2, ATTN_HEAD_DIM),
                v.reshape(b_, n, ATTN_HEADS, ATTN_V_DIM).transpose(0, 2, 1, 3))

    def to_heads(t):
        return t.transpose(0, 2, 3, 1, 4)

    q_lat, k_lat, v_lat = project(u_lat)
    q_lat, k_lat = axial_rope(q_lat), axial_rope(k_lat)
    q_ctx, k_ctx, v_ctx = project(u_ctx)
    k_ctx, q_ctx = to_heads(k_ctx), to_heads(q_ctx)
    k_all = jnp.concatenate([k_ctx, to_heads(k_lat)], axis=3)
    v_all = jnp.concatenate([v_ctx, v_lat], axis=2)

    n_blk = n_tok // Q_BLOCK
    q_blocks = jnp.moveaxis(to_heads(q_lat).reshape(b_, ATTN_HEADS, 2, n_blk, Q_BLOCK, ATTN_HEAD_DIM), 3, 0)
    o_blocks = lax.map(lambda qb: diff_attend(qb, k_all, v_all, lam), q_blocks)
    o_lat = jnp.moveaxis(o_blocks, 0, 2).reshape(b_, ATTN_HEADS, n_tok, ATTN_V_DIM)

    def finish(o):
        n = o.shape[2]
        o = rmsnorm(o, subln_w, SUBLN_EPS) * (1 - lambda_init)
        return o.transpose(0, 2, 1, 3).reshape(b_, n, ATTN_HEADS * ATTN_V_DIM) @ w_o

    y_lat = finish(o_lat)
    y_ctx = finish(diff_attend(q_ctx, k_ctx, v_ctx, lam)) if need_ctx else None
    return y_lat, y_ctx


def centred_depthwise_conv(x, w, b):
    y = lax.conv_general_dilated(x, w[:, None, :], window_strides=(1,),
                                 padding=[(CONV_W // 2, CONV_W // 2)],
                                 dimension_numbers=('NWC', 'WIO', 'NWC'),
                                 feature_group_count=x.shape[-1])
    return y + b


def headwise_blockdiag(x, w):
    xb = x.reshape(*x.shape[:-1], MLSTM_INNER // QKV_BLOCK, QKV_BLOCK)
    return jnp.einsum('blgi,gio->blgo', xb, w).reshape(x.shape)


def mlstm_chunk_scan(q, k, v, i_pre, log_f, state):
    b_, nh, n_tok, dh = q.shape
    n_chunks = n_tok // CHUNK

    def chunks(a):
        return jnp.moveaxis(a.reshape(b_, nh, n_chunks, CHUNK, *a.shape[3:]), 2, 0)

    in_order = jnp.tril(jnp.ones((CHUNK, CHUNK), dtype=bool))

    def step(carry, inp):
        c_mat, n_vec, m_prev = carry
        qc, kc, vc, ic, fc = inp
        b_cum = jnp.cumsum(fc, axis=-1)
        log_d = jnp.where(in_order, b_cum[..., :, None] - b_cum[..., None, :] + ic[..., None, :], -jnp.inf)
        m_inter = b_cum + m_prev[..., None]
        m_t = jnp.maximum(m_inter, jnp.max(log_d, axis=-1))
        s = jnp.einsum('bhtd,bhsd->bhts', qc, kc) * jnp.exp(log_d - m_t[..., None])
        w_inter = jnp.exp(m_inter - m_t)
        num = jnp.einsum('bhts,bhsv->bhtv', s, vc) + w_inter[..., None] * jnp.einsum('bhvk,bhtk->bhtv', c_mat, qc)
        den = jnp.sum(s, axis=-1) + w_inter * jnp.einsum('bhk,bhtk->bht', n_vec, qc)
        h = num / jnp.maximum(jnp.abs(den), jnp.exp(-m_t))[..., None]
        b_tot = b_cum[..., -1]
        log_w = b_tot[..., None] - b_cum + ic
        m_new = jnp.maximum(b_tot + m_prev, jnp.max(log_w, axis=-1))
        w_s = jnp.exp(log_w - m_new[..., None])
        decay = jnp.exp(b_tot + m_prev - m_new)
        c_new = decay[..., None, None] * c_mat + jnp.einsum('bhs,bhsv,bhsk->bhvk', w_s, vc, kc)
        n_new = decay[..., None] * n_vec + jnp.einsum('bhs,bhsk->bhk', w_s, kc)
        return (c_new, n_new, m_new), h

    final, h_c = lax.scan(step, state, (chunks(q), chunks(k), chunks(v), chunks(i_pre), chunks(log_f)))
    return jnp.moveaxis(h_c, 0, 2).reshape(q.shape), final


def mlstm_bidirectional(cell, st_f, st_b):
    q, k, v, ig_f, lf_f, ig_b, lf_b = cell
    fl = lambda a: jnp.flip(a, axis=2)
    h_f, end_f = mlstm_chunk_scan(q, k, v, ig_f, lf_f, st_f)
    h_b, end_b = mlstm_chunk_scan(fl(q), fl(k), fl(v), fl(ig_b), fl(lf_b), st_b)
    return h_f + fl(h_b), end_f, end_b


def mlstm_features(u, w_up, conv_w, conv_b, w_qkv, w_gates, b_gates):
    b_, n_tok, _ = u.shape
    x_in, z = jnp.split(u @ w_up, 2, axis=-1)
    x_conv = jax.nn.silu(centred_depthwise_conv(x_in, conv_w, conv_b))
    q = headwise_blockdiag(x_conv, w_qkv[0])
    k = headwise_blockdiag(x_conv, w_qkv[1])
    v = headwise_blockdiag(x_in, w_qkv[2])
    gates = (jnp.concatenate([q, k, v], axis=-1) @ w_gates + b_gates).astype(jnp.float32)
    ig_f, fg_f, ig_b, fg_b = jnp.split(gates.transpose(0, 2, 1), 4, axis=1)

    def heads(t):
        return t.reshape(b_, n_tok, MLSTM_HEADS, MLSTM_HEAD_DIM).transpose(0, 2, 1, 3).astype(jnp.float32)

    cell = (heads(q), heads(k) * (MLSTM_HEAD_DIM ** -0.5), heads(v),
            ig_f, jax.nn.log_sigmoid(fg_f), ig_b, jax.nn.log_sigmoid(fg_b))
    return cell, x_conv, z


def mlstm_output(h, x_conv, z, skip, norm_w, w_down):
    b_, nh, n_tok, dh = h.shape
    mu = jnp.mean(h, axis=-1, keepdims=True)
    var = jnp.mean(jnp.square(h - mu), axis=-1, keepdims=True)
    hn = (h - mu) * lax.rsqrt(var + HEAD_LN_EPS) * norm_w.astype(jnp.float32).reshape(nh, 1, dh)
    hn = hn.transpose(0, 2, 1, 3).reshape(b_, n_tok, nh * dh).astype(x_conv.dtype)
    return ((hn + skip * x_conv) * jax.nn.silu(z)) @ w_down


def mlstm_mixer(u_lat, u_ctx, w_up, conv_w, conv_b, w_qkv, w_gates, b_gates, skip, norm_w, w_down, need_ctx):
    b_ = u_lat.shape[0]
    cell_ctx, xc_ctx, z_ctx = mlstm_features(u_ctx, w_up, conv_w, conv_b, w_qkv, w_gates, b_gates)
    cell_lat, xc_lat, z_lat = mlstm_features(u_lat, w_up, conv_w, conv_b, w_qkv, w_gates, b_gates)
    zero = (jnp.zeros((b_, MLSTM_HEADS, MLSTM_HEAD_DIM, MLSTM_HEAD_DIM), jnp.float32),
            jnp.zeros((b_, MLSTM_HEADS, MLSTM_HEAD_DIM), jnp.float32),
            jnp.zeros((b_, MLSTM_HEADS), jnp.float32))
    h_ctx, st_f, st_b = mlstm_bidirectional(cell_ctx, zero, zero)
    h_lat, _, _ = mlstm_bidirectional(cell_lat, st_f, st_b)
    y_lat = mlstm_output(h_lat, xc_lat, z_lat, skip, norm_w, w_down)
    y_ctx = mlstm_output(h_ctx, xc_ctx, z_ctx, skip, norm_w, w_down) if need_ctx else None
    return y_lat, y_ctx


def setup_inputs(seed: int = 0) -> dict:
    key = jax.random.key(seed)
    ks = jax.random.split(key, 24)

    def nrm(k, shape, scale):
        return jax.random.normal(k, shape, jnp.float32) * scale

    forget_base = jnp.linspace(3.0, 6.0, MLSTM_HEADS, dtype=jnp.float32)
    zeros_h = jnp.zeros((MLSTM_HEADS,), jnp.float32)
    gate_base = jnp.concatenate([zeros_h, forget_base, zeros_h, forget_base])
    return {
        'x': nrm(ks[0], (BATCH, SEQ, D_MODEL), 1.0),
        'c': nrm(ks[1], (BATCH, D_MODEL), 1.0),
        'ctx': nrm(ks[2], (BATCH, CTX_LEN, D_MODEL), 1.0),
        'c_ctx': nrm(ks[3], (D_MODEL,), 1.0),
        'w_mod': nrm(ks[4], (DEPTH, D_MODEL, N_MOD * D_MODEL), 0.5 * D_MODEL ** -0.5),
        'b_mod': nrm(ks[5], (DEPTH, N_MOD * D_MODEL), 0.02),
        'norm_w': 1.0 + nrm(ks[6], (DEPTH, 3, D_MODEL), 0.02),
        'ffn_w_in': nrm(ks[7], (DEPTH, 2, D_MODEL, 2 * D_FF), D_MODEL ** -0.5),
        'ffn_w_out': nrm(ks[8], (DEPTH, 2, D_FF, D_MODEL), D_FF ** -0.5),
        'attn_w_qkv': nrm(ks[9], (N_ATTN_LAYERS, D_MODEL, 3 * D_MODEL), D_MODEL ** -0.5),
        'attn_w_o': nrm(ks[10], (N_ATTN_LAYERS, D_MODEL, D_MODEL), D_MODEL ** -0.5),
        'attn_lambda': nrm(ks[11], (N_ATTN_LAYERS, 4, ATTN_HEAD_DIM), 0.1),
        'attn_subln_w': 1.0 + nrm(ks[12], (N_ATTN_LAYERS, ATTN_V_DIM), 0.02),
        'mlstm_w_up': nrm(ks[13], (N_MLSTM_LAYERS, D_MODEL, 2 * MLSTM_INNER), D_MODEL ** -0.5),
        'mlstm_conv_w': nrm(ks[14], (N_MLSTM_LAYERS, CONV_W, MLSTM_INNER), CONV_W ** -0.5),
        'mlstm_conv_b': nrm(ks[15], (N_MLSTM_LAYERS, MLSTM_INNER), 0.02),
        'mlstm_w_qkv': nrm(ks[16], (N_MLSTM_LAYERS, 3, MLSTM_INNER // QKV_BLOCK, QKV_BLOCK, QKV_BLOCK), QKV_BLOCK ** -0.5),
        'mlstm_w_gates': nrm(ks[17], (N_MLSTM_LAYERS, 3 * MLSTM_INNER, 4 * MLSTM_HEADS), 0.1 * (3 * MLSTM_INNER) ** -0.5),
        'mlstm_b_gates': gate_base + nrm(ks[18], (N_MLSTM_LAYERS, 4 * MLSTM_HEADS), 0.1),
        'mlstm_skip': 1.0 + nrm(ks[19], (N_MLSTM_LAYERS, MLSTM_INNER), 0.02),
        'mlstm_norm_w': 1.0 + nrm(ks[20], (N_MLSTM_LAYERS, MLSTM_INNER), 0.02),
        'mlstm_w_down': nrm(ks[21], (N_MLSTM_LAYERS, MLSTM_INNER, D_MODEL), MLSTM_INNER ** -0.5),
        'final_norm_w': 1.0 + nrm(ks[22], (D_MODEL,), 0.02),
    }


def reference(x, c, ctx, c_ctx, w_mod, b_mod, norm_w, ffn_w_in, ffn_w_out,
              attn_w_qkv, attn_w_o, attn_lambda, attn_subln_w,
              mlstm_w_up, mlstm_conv_w, mlstm_conv_b, mlstm_w_qkv, mlstm_w_gates, mlstm_b_gates,
              mlstm_skip, mlstm_norm_w, mlstm_w_down, final_norm_w):
    b_, n_tok, d = x.shape
    sc = jax.nn.silu(c)
    scc = jax.nn.silu(c_ctx)
    h_lat, h_ctx = x, ctx
    for i in range(DEPTH):
        last = i == DEPTH - 1
        mod_lat = (sc @ w_mod[i] + b_mod[i]).reshape(b_, 1, N_MOD, d)
        mod_ctx = (scc @ w_mod[i] + b_mod[i]).reshape(1, 1, N_MOD, d)
        h_lat = half_ffn(h_lat, norm_w[i, 0], mod_lat, 0, ffn_w_in[i, 0], ffn_w_out[i, 0])
        h_ctx = half_ffn(h_ctx, norm_w[i, 0], mod_ctx, 0, ffn_w_in[i, 0], ffn_w_out[i, 0])
        u_lat = adaln_input(h_lat, norm_w[i, 1], mod_lat, 1)
        u_ctx = adaln_input(h_ctx, norm_w[i, 1], mod_ctx, 1)
        j = i // N_MIXERS
        if i % N_MIXERS == 0:
            y_lat, y_ctx = diff_attention_mixer(u_lat, u_ctx, attn_w_qkv[j], attn_w_o[j], attn_lambda[j],
                                                attn_subln_w[j], i, not last)
        else:
            y_lat, y_ctx = mlstm_mixer(u_lat, u_ctx, mlstm_w_up[j], mlstm_conv_w[j], mlstm_conv_b[j],
                                       mlstm_w_qkv[j], mlstm_w_gates[j], mlstm_b_gates[j], mlstm_skip[j],
                                       mlstm_norm_w[j], mlstm_w_down[j], not last)
        h_lat = h_lat + mod_lat[:, :, 5] * y_lat
        h_lat = half_ffn(h_lat, norm_w[i, 2], mod_lat, 2, ffn_w_in[i, 1], ffn_w_out[i, 1])
        if not last:
            h_ctx = h_ctx + mod_ctx[:, :, 5] * y_ctx
            h_ctx = half_ffn(h_ctx, norm_w[i, 2], mod_ctx, 2, ffn_w_in[i, 1], ffn_w_out[i, 1])
    return rmsnorm(h_lat, final_norm_w)
```

```python
import functools
import math

import jax
import jax.numpy as jnp
from jax import lax
from jax.experimental import pallas as pl
from jax.experimental.pallas import tpu as pltpu

N_MOD = 9
RMS_EPS = 1e-6
ATTN_HEADS = 8
ATTN_HEAD_DIM = 64
ATTN_V_DIM = 128
ROPE_THETA = 10000.0
GRID_W = 64
SUBLN_EPS = 1e-5
MLSTM_HEADS = 4
QKV_BLOCK = 4
CONV_W = 5
HEAD_LN_EPS = 1e-5
N_MIXERS = 2

LANES = 128
SUBLANES = 8
MXU_DIM = 256
MOD_ROWS_PAD = 8
VMEM_LIMIT = 56 * 1024 * 1024

BF16 = jnp.bfloat16
F32 = jnp.float32


def _cparams(n_axes):
    return pltpu.CompilerParams(
        dimension_semantics=("arbitrary",) * n_axes,
        vmem_limit_bytes=VMEM_LIMIT)


def _resident(shape):
    nd = len(shape)
    return pl.BlockSpec(shape, lambda *_: (0,) * nd, pipeline_mode=pl.Buffered(1))


def _dot(a, b):
    return jnp.dot(a, b, preferred_element_type=F32)


def _dot_nt(a, b):
    return lax.dot_general(a, b, (((1,), (1,)), ((), ())), preferred_element_type=F32)


def _dot_tn(a, b):
    return lax.dot_general(a, b, (((0,), (0,)), ((), ())), preferred_element_type=F32)


def _sigmoid(x):
    return 1.0 / (1.0 + jnp.exp(-x))


def _silu(x):
    return x * _sigmoid(x)


class _Rows:
    def __init__(self, batch, seq, ctx, tile):
        assert seq % tile == 0 and (batch * ctx) % tile == 0
        self.batch, self.seq, self.ctx, self.tile = batch, seq, ctx, tile
        self.n_lat = batch * seq
        self.rows = batch * (seq + ctx)
        self.lat_tiles = self.n_lat // tile
        self.tiles = self.rows // tile
        self.tiles_per_seq = seq // tile

    def mod_row(self, i):
        return jnp.where(i < self.lat_tiles, i // self.tiles_per_seq, self.batch)


def _mod_kernel(c_ref, w_ref, b_ref, o_ref):
    s = _silu(c_ref[...])
    o_ref[0] = jnp.dot(s, w_ref[0], preferred_element_type=F32,
                       precision=lax.Precision.HIGHEST) + b_ref[0]


def _modulation(cc, w_mod, b_mod):
    depth, d, nd = w_mod.shape
    rows = cc.shape[0]
    tn = d
    return pl.pallas_call(
        _mod_kernel,
        grid=(depth, nd // tn),
        in_specs=[pl.BlockSpec((rows, d), lambda l, j: (0, 0)),
                  pl.BlockSpec((1, d, tn), lambda l, j: (l, 0, j)),
                  pl.BlockSpec((1, 1, tn), lambda l, j: (l, 0, j))],
        out_specs=pl.BlockSpec((1, rows, tn), lambda l, j: (l, 0, j)),
        out_shape=jax.ShapeDtypeStruct((depth, rows, nd), F32),
        compiler_params=_cparams(2),
        name="modulation",
    )(cc, w_mod, b_mod.reshape(depth, 1, nd))


def _rms(x, w, eps):
    return x * lax.rsqrt(jnp.mean(x * x, axis=-1, keepdims=True) + eps) * w


def _ffn_chunks(d_ff):
    step = 3 * MXU_DIM
    out, c = [], 0
    while c < d_ff:
        w = min(step, d_ff - c)
        out.append((c, w))
        c += w
    return out


def _ffn_kernel(h_ref, mod_ref, nw_ref, win_ref, wout_ref, fin_ref, *out_refs,
                sub, d_ff, emit):
    x = h_ref[...]
    shift = mod_ref[0, 3 * sub:3 * sub + 1, :]
    scale = mod_ref[0, 3 * sub + 1:3 * sub + 2, :]
    gate = mod_ref[0, 3 * sub + 2:3 * sub + 3, :]
    u = (_rms(x, nw_ref[sub:sub + 1, :], RMS_EPS) * (1.0 + scale) + shift).astype(BF16)
    acc = jnp.zeros(x.shape, F32)
    for c0, cw in _ffn_chunks(d_ff):
        g = _dot(u, win_ref[:, c0:c0 + cw])
        up = _dot(u, win_ref[:, d_ff + c0:d_ff + c0 + cw])
        hid = (_silu(g) * up).astype(BF16)
        acc = acc + _dot(hid, wout_ref[c0:c0 + cw, :])
    y = x + 0.5 * gate * acc
    if emit == "mixer_in":
        out_refs[0][...] = y
        u2 = _rms(y, nw_ref[1:2, :], RMS_EPS) * (1.0 + mod_ref[0, 4:5, :]) + mod_ref[0, 3:4, :]
        out_refs[1][...] = u2.astype(BF16)
    elif emit == "final":
        out_refs[0][...] = _rms(y, fin_ref[...], RMS_EPS)
    else:
        out_refs[0][...] = y


def _half_ffn(rows, h, mod, nw, w_in, w_out, fin_w, *, sub, emit):
    d = h.shape[1]
    d_ff = w_out.shape[0]
    tm = rows.tile
    row_spec = pl.BlockSpec((tm, d), lambda i: (i, 0))
    out_shape = [jax.ShapeDtypeStruct(h.shape, F32)]
    out_specs = [row_spec]
    if emit == "mixer_in":
        out_shape.append(jax.ShapeDtypeStruct(h.shape, BF16))
        out_specs.append(row_spec)
    res = pl.pallas_call(
        functools.partial(_ffn_kernel, sub=sub, d_ff=d_ff, emit=emit),
        grid=(rows.tiles,),
        in_specs=[row_spec,
                  pl.BlockSpec((1, N_MOD, d), lambda i: (rows.mod_row(i), 0, 0)),
                  _resident(nw.shape),
                  _resident(w_in.shape),
                  _resident(w_out.shape),
                  _resident(fin_w.shape)],
        out_specs=out_specs,
        out_shape=out_shape,
        compiler_params=_cparams(1),
        name=f"half_ffn_{emit}",
    )(h, mod, nw, w_in, w_out, fin_w)
    return res


def _proj_kernel(a_ref, w_ref, h_ref, mod_ref, o_ref):
    o_ref[...] = h_ref[...] + mod_ref[0, 5:6, :] * _dot(a_ref[...], w_ref[...])


def _proj_residual(rows, a, w, h, mod):
    d = h.shape[1]
    ka = a.shape[1]
    tm = rows.tile
    return pl.pallas_call(
        _proj_kernel,
        grid=(rows.tiles,),
        in_specs=[pl.BlockSpec((tm, ka), lambda i: (i, 0)),
                  _resident(w.shape),
                  pl.BlockSpec((tm, d), lambda i: (i, 0)),
                  pl.BlockSpec((1, N_MOD, d), lambda i: (rows.mod_row(i), 0, 0))],
        out_specs=pl.BlockSpec((tm, d), lambda i: (i, 0)),
        out_shape=jax.ShapeDtypeStruct(h.shape, F32),
        compiler_params=_cparams(1),
        name="proj_residual",
    )(a, w, h, mod)


def _rope_tables(seq, pad_rows):
    pos = jnp.arange(seq)
    row_pos = (pos // GRID_W).astype(F32)
    col_pos = (pos % GRID_W).astype(F32)
    half = ATTN_HEAD_DIM // 2
    inv_freq = 1.0 / (ROPE_THETA ** (jnp.arange(0, half, 2, dtype=F32) / half))
    ang = jnp.stack([row_pos[:, None] * inv_freq, col_pos[:, None] * inv_freq], axis=1)
    cos = jnp.cos(ang)
    sin = jnp.sin(ang)
    cos64 = jnp.stack([cos, cos], axis=2).reshape(seq, ATTN_HEAD_DIM)
    sin64 = jnp.stack([-sin, sin], axis=2).reshape(seq, ATTN_HEAD_DIM)
    cos128 = jnp.concatenate([cos64, cos64], axis=1)
    sin128 = jnp.concatenate([sin64, sin64], axis=1)
    cos128 = jnp.concatenate([cos128, jnp.ones((pad_rows, LANES), F32)], axis=0)
    sin128 = jnp.concatenate([sin128, jnp.zeros((pad_rows, LANES), F32)], axis=0)
    return cos128, sin128


def _rope(x, cos, sin, first_half):
    fwd = pltpu.roll(x, LANES - 16, axis=1)
    bwd = pltpu.roll(x, 16, axis=1)
    return x * cos + jnp.where(first_half, fwd, bwd) * sin


def _qkv_kernel(u_ref, w_ref, cos_ref, sin_ref, q_ref, k_ref, v_ref):
    u = u_ref[...]
    d = u.shape[1]
    cos = cos_ref[...]
    sin = sin_ref[...]
    lane = lax.broadcasted_iota(jnp.int32, cos.shape, 1)
    first_half = (lane % 32) < 16
    q = _dot(u, w_ref[:, 0:d])
    k = _dot(u, w_ref[:, d:2 * d])
    qk_scale = ATTN_HEAD_DIM ** -0.5
    for hd in range(d // LANES):
        sl = slice(hd * LANES, (hd + 1) * LANES)
        q_ref[:, sl] = (_rope(q[:, sl], cos, sin, first_half) * qk_scale).astype(BF16)
        k_ref[:, sl] = _rope(k[:, sl], cos, sin, first_half).astype(BF16)
    v_ref[...] = _dot(u, w_ref[:, 2 * d:3 * d]).astype(BF16)


def _attn_qkv(rows, u, w_qkv, cos, sin):
    d = u.shape[1]
    tm = rows.tile
    row_spec = pl.BlockSpec((tm, d), lambda i: (i, 0))
    tab_spec = pl.BlockSpec(
        (tm, LANES),
        lambda i: (jnp.where(i < rows.lat_tiles, i % rows.tiles_per_seq, rows.tiles_per_seq), 0))
    return pl.pallas_call(
        _qkv_kernel,
        grid=(rows.tiles,),
        in_specs=[row_spec, _resident(w_qkv.shape), tab_spec, tab_spec],
        out_specs=[row_spec] * 3,
        out_shape=[jax.ShapeDtypeStruct(u.shape, BF16)] * 3,
        compiler_params=_cparams(1),
        name="attn_qkv",
    )(u, w_qkv, cos, sin)


def _attn_kernel(lam_ref, subln_ref, q_ref, *refs, n_seg, lambda_init):
    k_refs = refs[0:n_seg]
    v_refs = refs[n_seg:2 * n_seg]
    o_ref = refs[2 * n_seg]
    lv = lam_ref[...]
    lam = (jnp.exp(jnp.sum(lv[0:1] * lv[1:2], axis=-1, keepdims=True))
           - jnp.exp(jnp.sum(lv[2:3] * lv[3:4], axis=-1, keepdims=True)) + lambda_init)
    q = q_ref[...]
    lane = lax.broadcasted_iota(jnp.int32, q.shape, 1)
    zero = jnp.zeros_like(q)
    q_maps = (jnp.where(lane < ATTN_HEAD_DIM, q, zero), jnp.where(lane >= ATTN_HEAD_DIM, q, zero))
    probs = []
    for qm in q_maps:
        s = [_dot_nt(qm, k_ref[...]) for k_ref in k_refs]
        m = functools.reduce(jnp.maximum, [jnp.max(x, axis=-1, keepdims=True) for x in s])
        p = [jnp.exp(x - m) for x in s]
        l = functools.reduce(lambda a, b: a + b, [jnp.sum(x, axis=-1, keepdims=True) for x in p])
        probs.append((p, 1.0 / l))
    (p1, r1), (p2, r2) = probs
    r2 = lam * r2
    o = None
    for i in range(n_seg):
        a = (p1[i] * r1 - p2[i] * r2).astype(BF16)
        pv = _dot(a, v_refs[i][...])
        o = pv if o is None else o + pv
    o_ref[...] = (_rms(o, subln_ref[...], SUBLN_EPS) * (1.0 - lambda_init)).astype(BF16)


def _attention(q, k, v, lam, subln_w, *, batch, q_len, q_row0, segs, tq, lambda_init):
    d = q.shape[1]
    n_seg = len(segs)
    q_blk0 = q_row0 // tq
    q_tiles = q_len // tq

    def q_map(b, h, t):
        return (q_blk0 + b * q_tiles + t, h)

    def o_map(b, h, t):
        return (b * q_tiles + t, h)

    kv_specs = []
    for first, length in segs:
        assert first % length == 0
        kv_specs.append(pl.BlockSpec((length, LANES),
                                     lambda b, h, t, f=first // length: (f + b, h)))
    return pl.pallas_call(
        functools.partial(_attn_kernel, n_seg=n_seg, lambda_init=lambda_init),
        grid=(batch, ATTN_HEADS, q_tiles),
        in_specs=[pl.BlockSpec(lam.shape, lambda b, h, t: (0, 0)),
                  pl.BlockSpec(subln_w.shape, lambda b, h, t: (0, 0)),
                  pl.BlockSpec((tq, LANES), q_map)] + kv_specs + kv_specs,
        out_specs=pl.BlockSpec((tq, LANES), o_map),
        out_shape=jax.ShapeDtypeStruct((batch * q_len, d), BF16),
        compiler_params=_cparams(3),
        name=f"diff_attention_{n_seg}seg",
    )(lam, subln_w, q, *([k] * n_seg), *([v] * n_seg))


def _up_kernel(u_ref, w_ref, x_ref, z_ref):
    u = u_ref[...]
    inner = x_ref.shape[1]
    x_ref[...] = _dot(u, w_ref[:, 0:inner])
    z_ref[...] = _dot(u, w_ref[:, inner:2 * inner]).astype(BF16)


def _mlstm_up(rows, u, w_up):
    d = u.shape[1]
    inner = w_up.shape[1] // 2
    tm = rows.tile
    out_spec = pl.BlockSpec((tm, inner), lambda i: (i, 0))
    return pl.pallas_call(
        _up_kernel,
        grid=(rows.tiles,),
        in_specs=[pl.BlockSpec((tm, d), lambda i: (i, 0)), _resident(w_up.shape)],
        out_specs=[out_spec, out_spec],
        out_shape=[jax.ShapeDtypeStruct((rows.rows, inner), F32),
                   jax.ShapeDtypeStruct((rows.rows, inner), BF16)],
        compiler_params=_cparams(1),
        name="mlstm_up",
    )(u, w_up)


def _log_sigmoid(x):
    return jnp.minimum(x, 0.0) - jnp.log1p(jnp.exp(-jnp.abs(x)))


def _feat_kernel(x_ref, xp_ref, xn_ref, cw_ref, cb_ref, wq_ref, wk_ref, wv_ref, wg_ref, bg_ref,
                 q_ref, k_ref, v_ref, xc_ref, g_ref, gt_ref, xe_ref,
                 *, tf, lat_tiles, lat_tps, ctx_tps, k_scale):
    i = pl.program_id(0)
    j = jnp.where(i < lat_tiles, i % lat_tps, (i - lat_tiles) % ctx_tps)
    tps = jnp.where(i < lat_tiles, lat_tps, ctx_tps)
    first = j == 0
    last = j == tps - 1
    x = x_ref[...]
    halo = SUBLANES
    xe_ref[0:halo, :] = jnp.where(first, 0.0, xp_ref[...])
    xe_ref[halo:halo + tf, :] = x
    xe_ref[halo + tf:2 * halo + tf, :] = jnp.where(last, 0.0, xn_ref[...])
    conv = jnp.zeros(x.shape, F32) + cb_ref[...]
    for t in range(CONV_W):
        off = halo - CONV_W // 2 + t
        conv = conv + cw_ref[t:t + 1, :] * xe_ref[off:off + tf, :]
    xc = _silu(conv)
    xc_ref[...] = xc.astype(BF16)
    xc_b = xc.astype(BF16)
    x_b = x.astype(BF16)
    nblk = wq_ref.shape[0]
    bw = wq_ref.shape[1]
    inner = x.shape[1]
    gates = jnp.zeros((tf, LANES), F32) + bg_ref[...]
    for part, (src, w_ref, o_ref, scl) in enumerate(
            ((xc_b, wq_ref, q_ref, 1.0), (xc_b, wk_ref, k_ref, k_scale), (x_b, wv_ref, v_ref, 1.0))):
        for jb in range(nblk):
            sl = slice(jb * bw, (jb + 1) * bw)
            y = _dot(src[:, sl], w_ref[jb])
            yb = y.astype(BF16)
            o_ref[:, sl] = yb if scl == 1.0 else (y * scl).astype(BF16)
            gates = gates + _dot(yb, wg_ref[part * inner + jb * bw:part * inner + (jb + 1) * bw, :])
    lane = lax.broadcasted_iota(jnp.int32, gates.shape, 1)
    is_forget = (lane % (2 * MLSTM_HEADS)) >= MLSTM_HEADS
    gates = jnp.where(is_forget, _log_sigmoid(gates), gates)
    g1 = pltpu.roll(gates, LANES - 2 * MLSTM_HEADS, axis=1)
    g_ref[0] = gates
    g_ref[1] = g1
    gt_ref[0] = gates.T[0:2 * SUBLANES, :]
    gt_ref[1] = g1.T[0:2 * SUBLANES, :]


def _mlstm_features(rows_feat, x_in, conv_w, conv_b, wq, wk, wv, wg, bg, *, k_scale):
    r, inner = x_in.shape
    tf = rows_feat.tile
    halo = SUBLANES
    nh = r // halo
    main = pl.BlockSpec((tf, inner), lambda i: (i, 0))
    prev = pl.BlockSpec((halo, inner), lambda i: (jnp.maximum(i * (tf // halo) - 1, 0), 0))
    nxt = pl.BlockSpec((halo, inner), lambda i: (jnp.minimum((i + 1) * (tf // halo), nh - 1), 0))
    bf_out = jax.ShapeDtypeStruct((r, inner), BF16)
    return pl.pallas_call(
        functools.partial(_feat_kernel, tf=tf, lat_tiles=rows_feat.lat_tiles,
                          lat_tps=rows_feat.tiles_per_seq, ctx_tps=rows_feat.ctx // tf,
                          k_scale=k_scale),
        grid=(rows_feat.tiles,),
        in_specs=[main, prev, nxt, _resident(conv_w.shape), _resident(conv_b.shape),
                  _resident(wq.shape), _resident(wk.shape), _resident(wv.shape),
                  _resident(wg.shape), _resident(bg.shape)],
        out_specs=[main, main, main, main,
                   pl.BlockSpec((2, tf, LANES), lambda i: (0, i, 0)),
                   pl.BlockSpec((2, 2 * SUBLANES, tf), lambda i: (0, 0, i))],
        out_shape=[bf_out, bf_out, bf_out, bf_out,
                   jax.ShapeDtypeStruct((2, r, LANES), F32),
                   jax.ShapeDtypeStruct((2, 2 * SUBLANES, r), F32)],
        scratch_shapes=[pltpu.VMEM((tf + 2 * halo, inner), F32)],
        compiler_params=_cparams(1),
        name="mlstm_features",
    )(x_in, x_in, x_in, conv_w, conv_b, wq, wk, wv, wg, bg)


def _split3(x):
    a = x.astype(BF16)
    r = x - a.astype(F32)
    b = r.astype(BF16)
    c = (r - b.astype(F32)).astype(BF16)
    return a, b, c


def _scan_kernel(q_ref, k_ref, v_ref, g_ref, gt_ref, h_ref, c_scr, n_scr, m_scr, *, chunk, dh):
    d = pl.program_id(1)
    p = pl.program_id(2)

    @pl.when(p == 0)
    def _():
        c_scr[...] = jnp.zeros(c_scr.shape, F32)
        n_scr[...] = jnp.zeros(n_scr.shape, F32)
        m_scr[...] = jnp.zeros(m_scr.shape, F32)

    g = g_ref[0]
    gt = gt_ref[0]
    ti = lax.broadcasted_iota(jnp.int32, (chunk, chunk), 0)
    si = lax.broadcasted_iota(jnp.int32, (chunk, chunk), 1)
    sgn = 1 - 2 * d
    vis = (ti - si) * sgn >= 0
    tri = jnp.where(vis, 1.0, 0.0).astype(BF16)
    tri_t = jnp.where((si - ti) * sgn >= 0, 1.0, 0.0).astype(BF16)
    bc_all = functools.reduce(lambda a, b: a + b, [_dot(tri, t) for t in _split3(g)])
    br_all = functools.reduce(lambda a, b: a + b, [_dot(t, tri_t) for t in _split3(gt)])
    neg_inf = jnp.float32(-jnp.inf)
    for hd in range(MLSTM_HEADS):
        sl = slice(hd * dh, (hd + 1) * dh)
        q = q_ref[:, sl]
        k = k_ref[:, sl]
        v = v_ref[:, sl]
        ic = g[:, hd:hd + 1]
        fc = g[:, MLSTM_HEADS + hd:MLSTM_HEADS + hd + 1]
        bc = bc_all[:, MLSTM_HEADS + hd:MLSTM_HEADS + hd + 1]
        ir = gt[hd:hd + 1, :]
        br = br_all[MLSTM_HEADS + hd:MLSTM_HEADS + hd + 1, :]
        m_prev = m_scr[hd]
        c_mat = c_scr[hd]
        n_vec = n_scr[hd]
        log_d = jnp.where(vis, bc - br + ir, neg_inf)
        m_inter = bc + m_prev
        m_t = jnp.maximum(m_inter, jnp.max(log_d, axis=-1, keepdims=True))
        s = _dot_nt(q, k) * jnp.exp(log_d - m_t)
        w_inter = jnp.exp(m_inter - m_t)
        qf = q.astype(F32)
        num = _dot(s.astype(BF16), v) + w_inter * _dot_nt(q, c_mat.astype(BF16))
        den = jnp.sum(s, axis=-1, keepdims=True) + w_inter * jnp.sum(qf * n_vec, axis=-1, keepdims=True)
        h_ref[0, :, sl] = num / jnp.maximum(jnp.abs(den), jnp.exp(-m_t))
        b_tot = jnp.sum(fc, axis=0, keepdims=True)
        log_w = b_tot - bc + ic
        m_new = jnp.maximum(b_tot + m_prev, jnp.max(log_w, axis=0, keepdims=True))
        w_s = jnp.exp(log_w - m_new)
        decay = jnp.exp(b_tot + m_prev - m_new)
        vw = (v.astype(F32) * w_s).astype(BF16)
        c_scr[hd] = decay * c_mat + _dot_tn(vw, k)
        n_scr[hd] = decay * n_vec + jnp.sum(k.astype(F32) * w_s, axis=0, keepdims=True)
        m_scr[hd] = m_new


def _mlstm_scan(q, k, v, g, gt, *, batch, seq, ctx, chunk):
    r, inner = q.shape
    dh = inner // MLSTM_HEADS
    ncc = ctx // chunk
    ncl = seq // chunk
    lat_blocks = batch * ncl

    def blk(b, d, p):
        jc = jnp.where(d == 0, p, ncc - 1 - p)
        jl = jnp.where(d == 0, p - ncc, ncl - 1 - (p - ncc))
        return jnp.where(p < ncc, lat_blocks + b * ncc + jc, b * ncl + jl)

    row_spec = pl.BlockSpec((chunk, inner), lambda b, d, p: (blk(b, d, p), 0))
    return pl.pallas_call(
        functools.partial(_scan_kernel, chunk=chunk, dh=dh),
        grid=(batch, 2, ncc + ncl),
        in_specs=[row_spec, row_spec, row_spec,
                  pl.BlockSpec((1, chunk, LANES), lambda b, d, p: (d, blk(b, d, p), 0)),
                  pl.BlockSpec((1, 2 * SUBLANES, chunk), lambda b, d, p: (d, 0, blk(b, d, p)))],
        out_specs=pl.BlockSpec((1, chunk, inner), lambda b, d, p: (d, blk(b, d, p), 0)),
        out_shape=jax.ShapeDtypeStruct((2, r, inner), F32),
        scratch_shapes=[pltpu.VMEM((MLSTM_HEADS, dh, dh), F32),
                        pltpu.VMEM((MLSTM_HEADS, 1, dh), F32),
                        pltpu.VMEM((MLSTM_HEADS, 1, 1), F32)],
        compiler_params=_cparams(3),
        name="mlstm_scan",
    )(q, k, v, g, gt)


def _mlstm_out_kernel(hd_ref, xc_ref, z_ref, skip_ref, nw_ref, a_ref, *, dh):
    h = hd_ref[0] + hd_ref[1]
    for hd in range(MLSTM_HEADS):
        sl = slice(hd * dh, (hd + 1) * dh)
        hh = h[:, sl]
        mu = jnp.mean(hh, axis=-1, keepdims=True)
        cen = hh - mu
        var = jnp.mean(cen * cen, axis=-1, keepdims=True)
        hn = cen * lax.rsqrt(var + HEAD_LN_EPS) * nw_ref[:, sl]
        y = (hn + skip_ref[:, sl] * xc_ref[:, sl].astype(F32)) * _silu(z_ref[:, sl].astype(F32))
        a_ref[:, sl] = y.astype(BF16)


def _mlstm_out(rows, hdir, xc, z, skip, norm_w):
    r, inner = xc.shape
    tm = rows.tile
    row_spec = pl.BlockSpec((tm, inner), lambda i: (i, 0))
    return pl.pallas_call(
        functools.partial(_mlstm_out_kernel, dh=inner // MLSTM_HEADS),
        grid=(rows.tiles,),
        in_specs=[pl.BlockSpec((2, tm, inner), lambda i: (0, i, 0)), row_spec, row_spec,
                  _resident(skip.shape), _resident(norm_w.shape)],
        out_specs=row_spec,
        out_shape=jax.ShapeDtypeStruct((r, inner), BF16),
        compiler_params=_cparams(1),
        name="mlstm_out",
    )(hdir, xc, z, skip, norm_w)


def _blockdiag_dense(w, block):
    g = w.shape[0]
    per = block // QKV_BLOCK
    wt = w.reshape(g // per, per, QKV_BLOCK, QKV_BLOCK)
    eye = jnp.eye(per, dtype=w.dtype)
    dense = jnp.einsum("npio,pq->npiqo", wt, eye)
    return dense.reshape(g // per, block, block)


def _pick_tile(seq, ctx_rows, target):
    t = target
    while seq % t or ctx_rows % t:
        t //= 2
    return t


def kernel(x, c, ctx, c_ctx, w_mod, b_mod, norm_w, ffn_w_in, ffn_w_out,
           attn_w_qkv, attn_w_o, attn_lambda, attn_subln_w,
           mlstm_w_up, mlstm_conv_w, mlstm_conv_b, mlstm_w_qkv, mlstm_w_gates, mlstm_b_gates,
           mlstm_skip, mlstm_norm_w, mlstm_w_down, final_norm_w):
    batch, seq, d = x.shape
    ctx_len = ctx.shape[1]
    depth = w_mod.shape[0]
    inner = mlstm_w_down.shape[1]
    dh = inner // MLSTM_HEADS

    rows = _Rows(batch, seq, ctx_len, _pick_tile(seq, batch * ctx_len, 512))
    rows_feat = _Rows(batch, seq, ctx_len, _pick_tile(seq, ctx_len, 256))
    chunk = _pick_tile(seq, ctx_len, 128)
    tq = _pick_tile(seq, seq, 512)

    n_mod_rows = -(-(batch + 1) // MOD_ROWS_PAD) * MOD_ROWS_PAD
    cc = jnp.concatenate([c, c_ctx[None, :], jnp.zeros((n_mod_rows - batch - 1, d), F32)], axis=0)
    mod_all = _modulation(cc, w_mod, b_mod).reshape(depth, n_mod_rows, N_MOD, d)

    h = jnp.concatenate([x.reshape(batch * seq, d), ctx.reshape(batch * ctx_len, d)], axis=0)
    cos, sin = _rope_tables(seq, rows.tile)
    fin_w = final_norm_w.reshape(1, d)

    for i in range(depth):
        last = i == depth - 1
        j = i // N_MIXERS
        mod = mod_all[i]
        nw = norm_w[i]
        h, u = _half_ffn(rows, h, mod, nw, ffn_w_in[i, 0].astype(BF16), ffn_w_out[i, 0].astype(BF16),
                         fin_w, sub=0, emit="mixer_in")
        if i % N_MIXERS == 0:
            lambda_init = 0.8 - 0.6 * math.exp(-0.3 * i)
            q, k, v = _attn_qkv(rows, u, attn_w_qkv[j].astype(BF16), cos, sin)
            lam = attn_lambda[j]
            subln = attn_subln_w[j].reshape(1, ATTN_V_DIM)
            o = _attention(q, k, v, lam, subln, batch=batch, q_len=seq, q_row0=0,
                           segs=[(rows.n_lat, ctx_len), (0, seq)], tq=tq, lambda_init=lambda_init)
            o_ctx = _attention(q, k, v, lam, subln, batch=batch, q_len=ctx_len, q_row0=rows.n_lat,
                               segs=[(rows.n_lat, ctx_len)], tq=ctx_len, lambda_init=lambda_init)
            o = jnp.concatenate([o, o_ctx], axis=0)
            h = _proj_residual(rows, o, attn_w_o[j].astype(BF16), h, mod)
        else:
            x_in, z = _mlstm_up(rows, u, mlstm_w_up[j].astype(BF16))
            wq = _blockdiag_dense(mlstm_w_qkv[j, 0], MXU_DIM).astype(BF16)
            wk = _blockdiag_dense(mlstm_w_qkv[j, 1], MXU_DIM).astype(BF16)
            wv = _blockdiag_dense(mlstm_w_qkv[j, 2], MXU_DIM).astype(BF16)
            n_gate = mlstm_w_gates.shape[2]
            wg = jnp.pad(mlstm_w_gates[j], ((0, 0), (0, LANES - n_gate))).astype(BF16)
            bg = jnp.pad(mlstm_b_gates[j], (0, LANES - n_gate)).reshape(1, LANES)
            q, k, v, xc, g, gt = _mlstm_features(
                rows_feat, x_in, mlstm_conv_w[j], mlstm_conv_b[j].reshape(1, inner),
                wq, wk, wv, wg, bg, k_scale=dh ** -0.5)
            hdir = _mlstm_scan(q, k, v, g, gt, batch=batch, seq=seq, ctx=ctx_len, chunk=chunk)
            a = _mlstm_out(rows, hdir, xc, z, mlstm_skip[j].reshape(1, inner),
                           mlstm_norm_w[j].reshape(1, inner))
            h = _proj_residual(rows, a, mlstm_w_down[j].astype(BF16), h, mod)
        (h,) = _half_ffn(rows, h, mod, nw, ffn_w_in[i, 1].astype(BF16), ffn_w_out[i, 1].astype(BF16),
                         fin_w, sub=2, emit="final" if last else "plain")
    return h[:rows.n_lat].reshape(batch, seq, d)
```

```python
import functools
import math

import jax
import jax.numpy as jnp
from jax import lax
from jax.experimental import pallas as pl
from jax.experimental.pallas import tpu as pltpu

N_MOD = 9
RMS_EPS = 1e-6
ATTN_HEADS = 8
ATTN_HEAD_DIM = 64
ATTN_V_DIM = 128
ROPE_THETA = 10000.0
GRID_W = 64
SUBLN_EPS = 1e-5
MLSTM_HEADS = 4
QKV_BLOCK = 4
CONV_W = 5
HEAD_LN_EPS = 1e-5
N_MIXERS = 2

LANES = 128
SUBLANES = 8
MXU_DIM = 256
MOD_ROWS_PAD = 8
VMEM_LIMIT = 56 * 1024 * 1024

BF16 = jnp.bfloat16
F32 = jnp.float32


def _cparams(n_axes):
    return pltpu.CompilerParams(
        dimension_semantics=("arbitrary",) * n_axes,
        vmem_limit_bytes=VMEM_LIMIT)


def _resident(shape):
    nd = len(shape)
    return pl.BlockSpec(shape, lambda *_: (0,) * nd, pipeline_mode=pl.Buffered(1))


def _dot(a, b):
    return jnp.dot(a, b, preferred_element_type=F32)


def _dot_nt(a, b):
    return lax.dot_general(a, b, (((1,), (1,)), ((), ())), preferred_element_type=F32)


def _dot_tn(a, b):
    return lax.dot_general(a, b, (((0,), (0,)), ((), ())), preferred_element_type=F32)


def _sigmoid(x):
    return 1.0 / (1.0 + jnp.exp(-x))


def _silu(x):
    return x * _sigmoid(x)


class _Rows:
    def __init__(self, batch, seq, ctx, tile):
        assert seq % tile == 0 and (batch * ctx) % tile == 0
        self.batch, self.seq, self.ctx, self.tile = batch, seq, ctx, tile
        self.n_lat = batch * seq
        self.rows = batch * (seq + ctx)
        self.lat_tiles = self.n_lat // tile
        self.tiles = self.rows // tile
        self.tiles_per_seq = seq // tile

    def mod_row(self, i):
        return jnp.where(i < self.lat_tiles, i // self.tiles_per_seq, self.batch)

    def source_specs(self, n_src, width):
        tm = self.tile
        if n_src == 1:
            return [pl.BlockSpec((tm, width), lambda i: (i, 0))]
        lat_last = self.lat_tiles - 1
        return [pl.BlockSpec((tm, width), lambda i: (jnp.minimum(i, lat_last), 0)),
                pl.BlockSpec((tm, width), lambda i: (jnp.maximum(i - self.lat_tiles, 0), 0))]


def _pick_source(refs, lat_tiles):
    if len(refs) == 1:
        return refs[0][...]
    return jnp.where(pl.program_id(0) < lat_tiles, refs[0][...], refs[1][...])


def _mod_kernel(c_ref, w_ref, b_ref, o_ref):
    s = _silu(c_ref[...])
    o_ref[0] = jnp.dot(s, w_ref[0], preferred_element_type=F32,
                       precision=lax.Precision.HIGHEST) + b_ref[0]


def _modulation(cc, w_mod, b_mod):
    depth, d, nd = w_mod.shape
    rows = cc.shape[0]
    tn = d
    return pl.pallas_call(
        _mod_kernel,
        grid=(depth, nd // tn),
        in_specs=[pl.BlockSpec((rows, d), lambda l, j: (0, 0)),
                  pl.BlockSpec((1, d, tn), lambda l, j: (l, 0, j)),
                  pl.BlockSpec((1, 1, tn), lambda l, j: (l, 0, j))],
        out_specs=pl.BlockSpec((1, rows, tn), lambda l, j: (l, 0, j)),
        out_shape=jax.ShapeDtypeStruct((depth, rows, nd), F32),
        compiler_params=_cparams(2),
        name="modulation",
    )(cc, w_mod, b_mod.reshape(depth, 1, nd))


def _rms(x, w, eps):
    return x * lax.rsqrt(jnp.mean(x * x, axis=-1, keepdims=True) + eps) * w


def _ffn_chunks(d_ff):
    step = 3 * MXU_DIM
    out, c = [], 0
    while c < d_ff:
        w = min(step, d_ff - c)
        out.append((c, w))
        c += w
    return out


def _ffn_kernel(*refs, sub, d_ff, emit, n_h, n_a, lat_tiles):
    h_refs = refs[:n_h]
    mod_ref, nw_ref, win_ref, wout_ref, fin_ref = refs[n_h:n_h + 5]
    pos = n_h + 5
    a_refs = refs[pos:pos + n_a]
    pos += n_a
    if n_a:
        wp_ref = refs[pos]
        pos += 1
    out_refs = refs[pos:]

    x = _pick_source(h_refs, lat_tiles)
    if n_a:
        x = x + mod_ref[0, 5:6, :] * _dot(_pick_source(a_refs, lat_tiles), wp_ref[...])
    shift = mod_ref[0, 3 * sub:3 * sub + 1, :]
    scale = mod_ref[0, 3 * sub + 1:3 * sub + 2, :]
    gate = mod_ref[0, 3 * sub + 2:3 * sub + 3, :]
    u = (_rms(x, nw_ref[sub:sub + 1, :], RMS_EPS) * (1.0 + scale) + shift).astype(BF16)
    acc = jnp.zeros(x.shape, F32)
    for c0, cw in _ffn_chunks(d_ff):
        g = _dot(u, win_ref[:, c0:c0 + cw])
        up = _dot(u, win_ref[:, d_ff + c0:d_ff + c0 + cw])
        hid = (_silu(g) * up).astype(BF16)
        acc = acc + _dot(hid, wout_ref[c0:c0 + cw, :])
    y = x + 0.5 * gate * acc
    if emit == "mixer_in":
        out_refs[0][...] = y
        u2 = _rms(y, nw_ref[1:2, :], RMS_EPS) * (1.0 + mod_ref[0, 4:5, :]) + mod_ref[0, 3:4, :]
        out_refs[1][...] = u2.astype(BF16)
    elif emit == "final":
        out_refs[0][...] = _rms(y, fin_ref[...], RMS_EPS)
    else:
        out_refs[0][...] = y


def _half_ffn(rows, n_tiles, h_srcs, mod, nw, w_in, w_out, fin_w, *, sub, emit,
              a_srcs=(), w_proj=None):
    d = h_srcs[0].shape[1]
    d_ff = w_out.shape[0]
    tm = rows.tile
    row_spec = pl.BlockSpec((tm, d), lambda i: (i, 0))
    out_rows = n_tiles * tm
    out_shape = [jax.ShapeDtypeStruct((out_rows, d), F32)]
    out_specs = [row_spec]
    if emit == "mixer_in":
        out_shape.append(jax.ShapeDtypeStruct((out_rows, d), BF16))
        out_specs.append(row_spec)
    in_specs = (rows.source_specs(len(h_srcs), d)
                + [pl.BlockSpec((1, N_MOD, d), lambda i: (rows.mod_row(i), 0, 0)),
                   _resident(nw.shape), _resident(w_in.shape), _resident(w_out.shape),
                   _resident(fin_w.shape)])
    args = list(h_srcs) + [mod, nw, w_in, w_out, fin_w]
    if a_srcs:
        in_specs += rows.source_specs(len(a_srcs), a_srcs[0].shape[1]) + [_resident(w_proj.shape)]
        args += list(a_srcs) + [w_proj]
    return pl.pallas_call(
        functools.partial(_ffn_kernel, sub=sub, d_ff=d_ff, emit=emit, n_h=len(h_srcs),
                          n_a=len(a_srcs), lat_tiles=rows.lat_tiles),
        grid=(n_tiles,),
        in_specs=in_specs,
        out_specs=out_specs,
        out_shape=out_shape,
        compiler_params=_cparams(1),
        name=f"half_ffn_{emit}_{len(h_srcs)}h{len(a_srcs)}a",
    )(*args)


def _rope_tables(seq, pad_rows):
    pos = jnp.arange(seq)
    row_pos = (pos // GRID_W).astype(F32)
    col_pos = (pos % GRID_W).astype(F32)
    half = ATTN_HEAD_DIM // 2
    inv_freq = 1.0 / (ROPE_THETA ** (jnp.arange(0, half, 2, dtype=F32) / half))
    ang = jnp.stack([row_pos[:, None] * inv_freq, col_pos[:, None] * inv_freq], axis=1)
    cos = jnp.cos(ang)
    sin = jnp.sin(ang)
    cos64 = jnp.stack([cos, cos], axis=2).reshape(seq, ATTN_HEAD_DIM)
    sin64 = jnp.stack([-sin, sin], axis=2).reshape(seq, ATTN_HEAD_DIM)
    cos128 = jnp.concatenate([cos64, cos64], axis=1)
    sin128 = jnp.concatenate([sin64, sin64], axis=1)
    cos128 = jnp.concatenate([cos128, jnp.ones((pad_rows, LANES), F32)], axis=0)
    sin128 = jnp.concatenate([sin128, jnp.zeros((pad_rows, LANES), F32)], axis=0)
    return cos128, sin128


def _rope(x, cos, sin, first_half):
    fwd = pltpu.roll(x, LANES - 16, axis=1)
    bwd = pltpu.roll(x, 16, axis=1)
    return x * cos + jnp.where(first_half, fwd, bwd) * sin


def _qkv_kernel(u_ref, w_ref, cos_ref, sin_ref, q_ref, k_ref, v_ref):
    u = u_ref[...]
    d = u.shape[1]
    cos = cos_ref[...]
    sin = sin_ref[...]
    lane = lax.broadcasted_iota(jnp.int32, cos.shape, 1)
    first_half = (lane % 32) < 16
    q = _dot(u, w_ref[:, 0:d])
    k = _dot(u, w_ref[:, d:2 * d])
    v = _dot(u, w_ref[:, 2 * d:3 * d]).astype(BF16)
    qk_scale = ATTN_HEAD_DIM ** -0.5 * math.log2(math.e)
    ones = jnp.ones((u.shape[0], LANES), BF16)
    for hd in range(d // LANES):
        sl = slice(hd * LANES, (hd + 1) * LANES)
        q_ref[:, sl] = (_rope(q[:, sl], cos, sin, first_half) * qk_scale).astype(BF16)
        k_ref[:, sl] = _rope(k[:, sl], cos, sin, first_half).astype(BF16)
        v_ref[:, 2 * hd * LANES:(2 * hd + 1) * LANES] = v[:, sl]
        v_ref[:, (2 * hd + 1) * LANES:(2 * hd + 2) * LANES] = ones


def _attn_qkv(rows, u, w_qkv, cos, sin):
    d = u.shape[1]
    tm = rows.tile
    row_spec = pl.BlockSpec((tm, d), lambda i: (i, 0))
    tab_spec = pl.BlockSpec(
        (tm, LANES),
        lambda i: (jnp.where(i < rows.lat_tiles, i % rows.tiles_per_seq, rows.tiles_per_seq), 0))
    return pl.pallas_call(
        _qkv_kernel,
        grid=(rows.tiles,),
        in_specs=[row_spec, _resident(w_qkv.shape), tab_spec, tab_spec],
        out_specs=[row_spec, row_spec, pl.BlockSpec((tm, 2 * d), lambda i: (i, 0))],
        out_shape=[jax.ShapeDtypeStruct(u.shape, BF16), jax.ShapeDtypeStruct(u.shape, BF16),
                   jax.ShapeDtypeStruct((u.shape[0], 2 * d), BF16)],
        compiler_params=_cparams(1),
        name="attn_qkv",
    )(u, w_qkv, cos, sin)


def _attn_kernel(lam_ref, subln_ref, q_ref, *refs, n_seg, lambda_init):
    k_refs = refs[0:n_seg]
    v_refs = refs[n_seg:2 * n_seg]
    o_ref = refs[2 * n_seg]
    lv = lam_ref[...]
    lam = (jnp.exp(jnp.sum(lv[0:1] * lv[1:2], axis=-1, keepdims=True))
           - jnp.exp(jnp.sum(lv[2:3] * lv[3:4], axis=-1, keepdims=True)) + lambda_init)
    q = q_ref[...]
    lane = lax.broadcasted_iota(jnp.int32, q.shape, 1)
    zero = jnp.zeros_like(q)
    q_maps = (jnp.where(lane < ATTN_HEAD_DIM, q, zero), jnp.where(lane >= ATTN_HEAD_DIM, q, zero))
    outs = []
    for qm in q_maps:
        s = [_dot_nt(qm, k_ref[...]) for k_ref in k_refs]
        m = functools.reduce(jnp.maximum, [jnp.max(x, axis=-1, keepdims=True) for x in s])
        acc = None
        for x, v_ref in zip(s, v_refs):
            pv = _dot(jnp.exp2(x - m).astype(BF16), v_ref[...])
            acc = pv if acc is None else acc + pv
        outs.append(acc[:, 0:ATTN_V_DIM] / acc[:, ATTN_V_DIM:2 * ATTN_V_DIM])
    o = outs[0] - lam * outs[1]
    o_ref[...] = (_rms(o, subln_ref[...], SUBLN_EPS) * (1.0 - lambda_init)).astype(BF16)


def _attention(q, k, v, lam, subln_w, *, batch, q_len, q_row0, segs, tq, lambda_init):
    d = q.shape[1]
    n_seg = len(segs)
    q_blk0 = q_row0 // tq
    q_tiles = q_len // tq

    def q_map(b, h, t):
        return (q_blk0 + b * q_tiles + t, h)

    def o_map(b, h, t):
        return (b * q_tiles + t, h)

    k_specs, v_specs = [], []
    for first, length in segs:
        assert first % length == 0
        kv_map = lambda b, h, t, f=first // length: (f + b, h)
        k_specs.append(pl.BlockSpec((length, LANES), kv_map))
        v_specs.append(pl.BlockSpec((length, 2 * LANES), kv_map))
    return pl.pallas_call(
        functools.partial(_attn_kernel, n_seg=n_seg, lambda_init=lambda_init),
        grid=(batch, ATTN_HEADS, q_tiles),
        in_specs=[pl.BlockSpec(lam.shape, lambda b, h, t: (0, 0)),
                  pl.BlockSpec(subln_w.shape, lambda b, h, t: (0, 0)),
                  pl.BlockSpec((tq, LANES), q_map)] + k_specs + v_specs,
        out_specs=pl.BlockSpec((tq, LANES), o_map),
        out_shape=jax.ShapeDtypeStruct((batch * q_len, d), BF16),
        compiler_params=_cparams(3),
        name=f"diff_attention_{n_seg}seg",
    )(lam, subln_w, q, *([k] * n_seg), *([v] * n_seg))


def _up_kernel(u_ref, w_ref, x_ref, z_ref):
    u = u_ref[...]
    inner = x_ref.shape[1]
    x_ref[...] = _dot(u, w_ref[:, 0:inner])
    z_ref[...] = _dot(u, w_ref[:, inner:2 * inner]).astype(BF16)


def _mlstm_up(rows, u, w_up):
    d = u.shape[1]
    inner = w_up.shape[1] // 2
    tm = rows.tile
    out_spec = pl.BlockSpec((tm, inner), lambda i: (i, 0))
    return pl.pallas_call(
        _up_kernel,
        grid=(rows.tiles,),
        in_specs=[pl.BlockSpec((tm, d), lambda i: (i, 0)), _resident(w_up.shape)],
        out_specs=[out_spec, out_spec],
        out_shape=[jax.ShapeDtypeStruct((rows.rows, inner), F32),
                   jax.ShapeDtypeStruct((rows.rows, inner), BF16)],
        compiler_params=_cparams(1),
        name="mlstm_up",
    )(u, w_up)


def _log_sigmoid(x):
    return jnp.minimum(x, 0.0) - jnp.log1p(jnp.exp(-jnp.abs(x)))


def _feat_kernel(x_ref, xp_ref, xn_ref, cw_ref, cb_ref, wq_ref, wk_ref, wv_ref, wg_ref, bg_ref,
                 q_ref, k_ref, v_ref, xc_ref, g_ref, gt_ref, xe_ref,
                 *, tf, lat_tiles, lat_tps, ctx_tps, k_scale):
    i = pl.program_id(0)
    j = jnp.where(i < lat_tiles, i % lat_tps, (i - lat_tiles) % ctx_tps)
    tps = jnp.where(i < lat_tiles, lat_tps, ctx_tps)
    first = j == 0
    last = j == tps - 1
    x = x_ref[...]
    halo = SUBLANES
    xe_ref[0:halo, :] = jnp.where(first, 0.0, xp_ref[...])
    xe_ref[halo:halo + tf, :] = x
    xe_ref[halo + tf:2 * halo + tf, :] = jnp.where(last, 0.0, xn_ref[...])
    conv = jnp.zeros(x.shape, F32) + cb_ref[...]
    for t in range(CONV_W):
        off = halo - CONV_W // 2 + t
        conv = conv + cw_ref[t:t + 1, :] * xe_ref[off:off + tf, :]
    xc = _silu(conv)
    xc_ref[...] = xc.astype(BF16)
    xc_b = xc.astype(BF16)
    x_b = x.astype(BF16)
    nblk = wq_ref.shape[0]
    bw = wq_ref.shape[1]
    inner = x.shape[1]
    gates = jnp.zeros((tf, LANES), F32) + bg_ref[...]
    for part, (src, w_ref, o_ref, scl) in enumerate(
            ((xc_b, wq_ref, q_ref, 1.0), (xc_b, wk_ref, k_ref, k_scale), (x_b, wv_ref, v_ref, 1.0))):
        for jb in range(nblk):
            sl = slice(jb * bw, (jb + 1) * bw)
            y = _dot(src[:, sl], w_ref[jb])
            yb = y.astype(BF16)
            o_ref[:, sl] = yb if scl == 1.0 else (y * scl).astype(BF16)
            gates = gates + _dot(yb, wg_ref[part * inner + jb * bw:part * inner + (jb + 1) * bw, :])
    lane = lax.broadcasted_iota(jnp.int32, gates.shape, 1)
    is_forget = (lane % (2 * MLSTM_HEADS)) >= MLSTM_HEADS
    gates = jnp.where(is_forget, _log_sigmoid(gates), gates)
    g1 = pltpu.roll(gates, LANES - 2 * MLSTM_HEADS, axis=1)
    g_ref[0] = gates
    g_ref[1] = g1
    gt_ref[0] = gates.T[0:2 * SUBLANES, :]
    gt_ref[1] = g1.T[0:2 * SUBLANES, :]


def _mlstm_features(rows_feat, x_in, conv_w, conv_b, wq, wk, wv, wg, bg, *, k_scale):
    r, inner = x_in.shape
    tf = rows_feat.tile
    halo = SUBLANES
    nh = r // halo
    main = pl.BlockSpec((tf, inner), lambda i: (i, 0))
    prev = pl.BlockSpec((halo, inner), lambda i: (jnp.maximum(i * (tf // halo) - 1, 0), 0))
    nxt = pl.BlockSpec((halo, inner), lambda i: (jnp.minimum((i + 1) * (tf // halo), nh - 1), 0))
    bf_out = jax.ShapeDtypeStruct((r, inner), BF16)
    return pl.pallas_call(
        functools.partial(_feat_kernel, tf=tf, lat_tiles=rows_feat.lat_tiles,
                          lat_tps=rows_feat.tiles_per_seq, ctx_tps=rows_feat.ctx // tf,
                          k_scale=k_scale),
        grid=(rows_feat.tiles,),
        in_specs=[main, prev, nxt, _resident(conv_w.shape), _resident(conv_b.shape),
                  _resident(wq.shape), _resident(wk.shape), _resident(wv.shape),
                  _resident(wg.shape), _resident(bg.shape)],
        out_specs=[main, main, main, main,
                   pl.BlockSpec((2, tf, LANES), lambda i: (0, i, 0)),
                   pl.BlockSpec((2, 2 * SUBLANES, tf), lambda i: (0, 0, i))],
        out_shape=[bf_out, bf_out, bf_out, bf_out,
                   jax.ShapeDtypeStruct((2, r, LANES), F32),
                   jax.ShapeDtypeStruct((2, 2 * SUBLANES, r), F32)],
        scratch_shapes=[pltpu.VMEM((tf + 2 * halo, inner), F32)],
        compiler_params=_cparams(1),
        name="mlstm_features",
    )(x_in, x_in, x_in, conv_w, conv_b, wq, wk, wv, wg, bg)


def _split3(x):
    a = x.astype(BF16)
    r = x - a.astype(F32)
    b = r.astype(BF16)
    c = (r - b.astype(F32)).astype(BF16)
    return a, b, c


def _scan_kernel(q_ref, k_ref, v_ref, g_ref, gt_ref, h_ref, c_scr, cb_scr, n_scr, m_scr,
                 *, chunk, dh):
    d = pl.program_id(1)
    p = pl.program_id(2)

    @pl.when(p == 0)
    def _():
        c_scr[...] = jnp.zeros(c_scr.shape, F32)
        cb_scr[...] = jnp.zeros(cb_scr.shape, BF16)
        n_scr[...] = jnp.zeros(n_scr.shape, F32)
        m_scr[...] = jnp.zeros(m_scr.shape, F32)

    g = g_ref[0]
    gt = gt_ref[0]
    ti = lax.broadcasted_iota(jnp.int32, (chunk, chunk), 0)
    si = lax.broadcasted_iota(jnp.int32, (chunk, chunk), 1)
    sgn = 1 - 2 * d
    vis = (ti - si) * sgn >= 0
    tri = jnp.where(vis, 1.0, 0.0).astype(BF16)
    tri_t = jnp.where((si - ti) * sgn >= 0, 1.0, 0.0).astype(BF16)
    bc_all = functools.reduce(lambda a, b: a + b, [_dot(tri, t) for t in _split3(g)])
    br_all = functools.reduce(lambda a, b: a + b, [_dot(t, tri_t) for t in _split3(gt)])
    neg_inf = jnp.float32(-jnp.inf)
    for hd in range(MLSTM_HEADS):
        sl = slice(hd * dh, (hd + 1) * dh)
        q = q_ref[:, sl]
        k = k_ref[:, sl]
        v = v_ref[:, sl]
        ic = g[:, hd:hd + 1]
        fc = g[:, MLSTM_HEADS + hd:MLSTM_HEADS + hd + 1]
        bc = bc_all[:, MLSTM_HEADS + hd:MLSTM_HEADS + hd + 1]
        ir = gt[hd:hd + 1, :]
        br = br_all[MLSTM_HEADS + hd:MLSTM_HEADS + hd + 1, :]
        m_prev = m_scr[hd]
        n_vec = n_scr[hd]
        log_d = jnp.where(vis, bc - br + ir, neg_inf)
        m_inter = bc + m_prev
        m_t = jnp.maximum(m_inter, jnp.max(log_d, axis=-1, keepdims=True))
        s = _dot_nt(q, k) * jnp.exp(log_d - m_t)
        w_inter = jnp.exp(m_inter - m_t)
        qf = q.astype(F32)
        num = _dot(s.astype(BF16), v) + w_inter * _dot_nt(q, cb_scr[hd])
        den = jnp.sum(s, axis=-1, keepdims=True) + w_inter * jnp.sum(qf * n_vec, axis=-1, keepdims=True)
        h_ref[0, :, sl] = (num / jnp.maximum(jnp.abs(den), jnp.exp(-m_t))).astype(h_ref.dtype)
        b_tot = jnp.sum(fc, axis=0, keepdims=True)
        log_w = b_tot - bc + ic
        m_new = jnp.maximum(b_tot + m_prev, jnp.max(log_w, axis=0, keepdims=True))
        w_s = jnp.exp(log_w - m_new)
        decay = jnp.exp(b_tot + m_prev - m_new)
        vw = (v.astype(F32) * w_s).astype(BF16)
        c_new = decay * c_scr[hd] + _dot_tn(vw, k)
        c_scr[hd] = c_new
        cb_scr[hd] = c_new.astype(BF16)
        n_scr[hd] = decay * n_vec + jnp.sum(k.astype(F32) * w_s, axis=0, keepdims=True)
        m_scr[hd] = m_new


def _mlstm_scan(q, k, v, g, gt, *, batch, seq, ctx, chunk):
    r, inner = q.shape
    dh = inner // MLSTM_HEADS
    ncc = ctx // chunk
    ncl = seq // chunk
    lat_blocks = batch * ncl

    def blk(b, d, p):
        jc = jnp.where(d == 0, p, ncc - 1 - p)
        jl = jnp.where(d == 0, p - ncc, ncl - 1 - (p - ncc))
        return jnp.where(p < ncc, lat_blocks + b * ncc + jc, b * ncl + jl)

    row_spec = pl.BlockSpec((chunk, inner), lambda b, d, p: (blk(b, d, p), 0))
    return pl.pallas_call(
        functools.partial(_scan_kernel, chunk=chunk, dh=dh),
        grid=(batch, 2, ncc + ncl),
        in_specs=[row_spec, row_spec, row_spec,
                  pl.BlockSpec((1, chunk, LANES), lambda b, d, p: (d, blk(b, d, p), 0)),
                  pl.BlockSpec((1, 2 * SUBLANES, chunk), lambda b, d, p: (d, 0, blk(b, d, p)))],
        out_specs=pl.BlockSpec((1, chunk, inner), lambda b, d, p: (d, blk(b, d, p), 0)),
        out_shape=jax.ShapeDtypeStruct((2, r, inner), BF16),
        scratch_shapes=[pltpu.VMEM((MLSTM_HEADS, dh, dh), F32),
                        pltpu.VMEM((MLSTM_HEADS, dh, dh), BF16),
                        pltpu.VMEM((MLSTM_HEADS, 1, dh), F32),
                        pltpu.VMEM((MLSTM_HEADS, 1, 1), F32)],
        compiler_params=_cparams(3),
        name="mlstm_scan",
    )(q, k, v, g, gt)


def _mlstm_out_kernel(hd_ref, xc_ref, z_ref, skip_ref, nw_ref, a_ref, *, dh):
    for hd in range(MLSTM_HEADS):
        sl = slice(hd * dh, (hd + 1) * dh)
        hh = hd_ref[0, :, sl].astype(F32) + hd_ref[1, :, sl].astype(F32)
        mu = jnp.mean(hh, axis=-1, keepdims=True)
        cen = hh - mu
        var = jnp.mean(cen * cen, axis=-1, keepdims=True)
        hn = cen * lax.rsqrt(var + HEAD_LN_EPS) * nw_ref[:, sl]
        y = (hn + skip_ref[:, sl] * xc_ref[:, sl].astype(F32)) * _silu(z_ref[:, sl].astype(F32))
        a_ref[:, sl] = y.astype(BF16)


def _mlstm_out(rows, n_tiles, hdir, xc, z, skip, norm_w):
    inner = xc.shape[1]
    tm = rows.tile
    row_spec = pl.BlockSpec((tm, inner), lambda i: (i, 0))
    return pl.pallas_call(
        functools.partial(_mlstm_out_kernel, dh=inner // MLSTM_HEADS),
        grid=(n_tiles,),
        in_specs=[pl.BlockSpec((2, tm, inner), lambda i: (0, i, 0)), row_spec, row_spec,
                  _resident(skip.shape), _resident(norm_w.shape)],
        out_specs=row_spec,
        out_shape=jax.ShapeDtypeStruct((n_tiles * tm, inner), BF16),
        compiler_params=_cparams(1),
        name="mlstm_out",
    )(hdir, xc, z, skip, norm_w)


def _blockdiag_dense(w, block):
    g = w.shape[0]
    per = block // QKV_BLOCK
    wt = w.reshape(g // per, per, QKV_BLOCK, QKV_BLOCK)
    eye = jnp.eye(per, dtype=w.dtype)
    dense = jnp.einsum("npio,pq->npiqo", wt, eye)
    return dense.reshape(g // per, block, block)


def _pick_tile(seq, ctx_rows, target):
    t = target
    while seq % t or ctx_rows % t:
        t //= 2
    return t


def kernel(x, c, ctx, c_ctx, w_mod, b_mod, norm_w, ffn_w_in, ffn_w_out,
           attn_w_qkv, attn_w_o, attn_lambda, attn_subln_w,
           mlstm_w_up, mlstm_conv_w, mlstm_conv_b, mlstm_w_qkv, mlstm_w_gates, mlstm_b_gates,
           mlstm_skip, mlstm_norm_w, mlstm_w_down, final_norm_w):
    batch, seq, d = x.shape
    ctx_len = ctx.shape[1]
    depth = w_mod.shape[0]
    inner = mlstm_w_down.shape[1]
    dh = inner // MLSTM_HEADS

    rows = _Rows(batch, seq, ctx_len, _pick_tile(seq, batch * ctx_len, 512))
    rows_feat = _Rows(batch, seq, ctx_len, _pick_tile(seq, ctx_len, 256))
    chunk = _pick_tile(seq, ctx_len, 256)
    tq = _pick_tile(seq, seq, 512)

    n_mod_rows = -(-(batch + 1) // MOD_ROWS_PAD) * MOD_ROWS_PAD
    cc = jnp.concatenate([c, c_ctx[None, :], jnp.zeros((n_mod_rows - batch - 1, d), F32)], axis=0)
    mod_all = _modulation(cc, w_mod, b_mod).reshape(depth, n_mod_rows, N_MOD, d)

    cos, sin = _rope_tables(seq, rows.tile)
    fin_w = final_norm_w.reshape(1, d)
    h_srcs = (x.reshape(batch * seq, d), ctx.reshape(batch * ctx_len, d))

    for i in range(depth):
        last = i == depth - 1
        j = i // N_MIXERS
        mod = mod_all[i]
        nw = norm_w[i]
        h, u = _half_ffn(rows, rows.tiles, h_srcs, mod, nw,
                         ffn_w_in[i, 0].astype(BF16), ffn_w_out[i, 0].astype(BF16),
                         fin_w, sub=0, emit="mixer_in")
        n_post = rows.lat_tiles if last else rows.tiles
        if i % N_MIXERS == 0:
            lambda_init = 0.8 - 0.6 * math.exp(-0.3 * i)
            q, k, v = _attn_qkv(rows, u, attn_w_qkv[j].astype(BF16), cos, sin)
            lam = attn_lambda[j]
            subln = attn_subln_w[j].reshape(1, ATTN_V_DIM)
            o_lat = _attention(q, k, v, lam, subln, batch=batch, q_len=seq, q_row0=0,
                               segs=[(rows.n_lat, ctx_len), (0, seq)], tq=tq,
                               lambda_init=lambda_init)
            o_ctx = _attention(q, k, v, lam, subln, batch=batch, q_len=ctx_len, q_row0=rows.n_lat,
                               segs=[(rows.n_lat, ctx_len)], tq=ctx_len, lambda_init=lambda_init)
            a_srcs = (o_lat, o_ctx)
            w_proj = attn_w_o[j].astype(BF16)
        else:
            x_in, z = _mlstm_up(rows, u, mlstm_w_up[j].astype(BF16))
            wq = _blockdiag_dense(mlstm_w_qkv[j, 0], MXU_DIM).astype(BF16)
            wk = _blockdiag_dense(mlstm_w_qkv[j, 1], MXU_DIM).astype(BF16)
            wv = _blockdiag_dense(mlstm_w_qkv[j, 2], MXU_DIM).astype(BF16)
            n_gate = mlstm_w_gates.shape[2]
            wg = jnp.pad(mlstm_w_gates[j], ((0, 0), (0, LANES - n_gate))).astype(BF16)
            bg = jnp.pad(mlstm_b_gates[j], (0, LANES - n_gate)).reshape(1, LANES)
            q, k, v, xc, g, gt = _mlstm_features(
                rows_feat, x_in, mlstm_conv_w[j], mlstm_conv_b[j].reshape(1, inner),
                wq, wk, wv, wg, bg, k_scale=dh ** -0.5)
            hdir = _mlstm_scan(q, k, v, g, gt, batch=batch, seq=seq, ctx=ctx_len, chunk=chunk)
            a = _mlstm_out(rows, n_post, hdir, xc, z, mlstm_skip[j].reshape(1, inner),
                           mlstm_norm_w[j].reshape(1, inner))
            a_srcs = (a,)
            w_proj = mlstm_w_down[j].astype(BF16)
        (h,) = _half_ffn(rows, n_post, (h,), mod, nw,
                         ffn_w_in[i, 1].astype(BF16), ffn_w_out[i, 1].astype(BF16),
                         fin_w, sub=2, emit="final" if last else "plain",
                         a_srcs=a_srcs, w_proj=w_proj)
        h_srcs = (h,)
    return h.reshape(batch, seq, d)
```

```python
import functools
import math

import jax
import jax.numpy as jnp
from jax import lax
from jax.experimental import pallas as pl
from jax.experimental.pallas import tpu as pltpu

N_MOD = 9
RMS_EPS = 1e-6
ATTN_HEADS = 8
ATTN_HEAD_DIM = 64
ATTN_V_DIM = 128
ROPE_THETA = 10000.0
GRID_W = 64
SUBLN_EPS = 1e-5
MLSTM_HEADS = 4
QKV_BLOCK = 4
CONV_W = 5
HEAD_LN_EPS = 1e-5
N_MIXERS = 2

LANES = 128
SUBLANES = 8
MXU_DIM = 256
MOD_ROWS_PAD = 8
ATTN_CHAIN_ROWS = 128
VMEM_LIMIT = 56 * 1024 * 1024

BF16 = jnp.bfloat16
F32 = jnp.float32


def _cparams(n_axes):
    return pltpu.CompilerParams(
        dimension_semantics=("arbitrary",) * n_axes,
        vmem_limit_bytes=VMEM_LIMIT)


def _resident(shape):
    nd = len(shape)
    return pl.BlockSpec(shape, lambda *_: (0,) * nd, pipeline_mode=pl.Buffered(1))


def _dot(a, b):
    return jnp.dot(a, b, preferred_element_type=F32)


def _dot_nt(a, b):
    return lax.dot_general(a, b, (((1,), (1,)), ((), ())), preferred_element_type=F32)


def _dot_tn(a, b):
    return lax.dot_general(a, b, (((0,), (0,)), ((), ())), preferred_element_type=F32)


def _sigmoid(x):
    return 1.0 / (1.0 + jnp.exp(-x))


def _silu(x):
    return x * _sigmoid(x)


class _Rows:
    def __init__(self, batch, seq, ctx, tile):
        assert seq % tile == 0 and (batch * ctx) % tile == 0
        self.batch, self.seq, self.ctx, self.tile = batch, seq, ctx, tile
        self.n_lat = batch * seq
        self.rows = batch * (seq + ctx)
        self.lat_tiles = self.n_lat // tile
        self.tiles = self.rows // tile
        self.tiles_per_seq = seq // tile

    def mod_row(self, i):
        return jnp.where(i < self.lat_tiles, i // self.tiles_per_seq, self.batch)

    def source_specs(self, n_src, width):
        tm = self.tile
        if n_src == 1:
            return [pl.BlockSpec((tm, width), lambda i: (i, 0))]
        lat_last = self.lat_tiles - 1
        return [pl.BlockSpec((tm, width), lambda i: (jnp.minimum(i, lat_last), 0)),
                pl.BlockSpec((tm, width), lambda i: (jnp.maximum(i - self.lat_tiles, 0), 0))]


def _pick_source(refs, lat_tiles):
    if len(refs) == 1:
        return refs[0][...]
    return jnp.where(pl.program_id(0) < lat_tiles, refs[0][...], refs[1][...])


def _mod_kernel(c_ref, w_ref, b_ref, o_ref):
    s = _silu(c_ref[...])
    o_ref[0] = jnp.dot(s, w_ref[0], preferred_element_type=F32,
                       precision=lax.Precision.HIGHEST) + b_ref[0]


def _modulation(cc, w_mod, b_mod):
    depth, d, nd = w_mod.shape
    rows = cc.shape[0]
    tn = d
    return pl.pallas_call(
        _mod_kernel,
        grid=(depth, nd // tn),
        in_specs=[pl.BlockSpec((rows, d), lambda l, j: (0, 0)),
                  pl.BlockSpec((1, d, tn), lambda l, j: (l, 0, j)),
                  pl.BlockSpec((1, 1, tn), lambda l, j: (l, 0, j))],
        out_specs=pl.BlockSpec((1, rows, tn), lambda l, j: (l, 0, j)),
        out_shape=jax.ShapeDtypeStruct((depth, rows, nd), F32),
        compiler_params=_cparams(2),
        name="modulation",
    )(cc, w_mod, b_mod.reshape(depth, 1, nd))


def _rms(x, w, eps):
    return x * lax.rsqrt(jnp.mean(x * x, axis=-1, keepdims=True) + eps) * w


def _ffn_chunks(d_ff):
    step = 3 * MXU_DIM
    out, c = [], 0
    while c < d_ff:
        w = min(step, d_ff - c)
        out.append((c, w))
        c += w
    return out


def _ffn_kernel(*refs, sub, d_ff, emit, n_h, n_a, lat_tiles):
    h_refs = refs[:n_h]
    mod_ref, nw_ref, win_ref, wout_ref, fin_ref = refs[n_h:n_h + 5]
    pos = n_h + 5
    a_refs = refs[pos:pos + n_a]
    pos += n_a
    if n_a:
        wp_ref = refs[pos]
        pos += 1
    out_refs = refs[pos:]

    x = _pick_source(h_refs, lat_tiles)
    if n_a:
        x = x + mod_ref[0, 5:6, :] * _dot(_pick_source(a_refs, lat_tiles), wp_ref[...])
    shift = mod_ref[0, 3 * sub:3 * sub + 1, :]
    scale = mod_ref[0, 3 * sub + 1:3 * sub + 2, :]
    gate = mod_ref[0, 3 * sub + 2:3 * sub + 3, :]
    u = (_rms(x, nw_ref[sub:sub + 1, :], RMS_EPS) * (1.0 + scale) + shift).astype(BF16)
    acc = jnp.zeros(x.shape, F32)
    for c0, cw in _ffn_chunks(d_ff):
        g = _dot(u, win_ref[:, c0:c0 + cw])
        up = _dot(u, win_ref[:, d_ff + c0:d_ff + c0 + cw])
        hid = (_silu(g) * up).astype(BF16)
        acc = acc + _dot(hid, wout_ref[c0:c0 + cw, :])
    y = x + 0.5 * gate * acc
    if emit == "mixer_in":
        out_refs[0][...] = y
        u2 = _rms(y, nw_ref[1:2, :], RMS_EPS) * (1.0 + mod_ref[0, 4:5, :]) + mod_ref[0, 3:4, :]
        out_refs[1][...] = u2.astype(BF16)
    elif emit == "final":
        out_refs[0][...] = _rms(y, fin_ref[...], RMS_EPS)
    else:
        out_refs[0][...] = y


def _half_ffn(rows, n_tiles, h_srcs, mod, nw, w_in, w_out, fin_w, *, sub, emit,
              a_srcs=(), w_proj=None):
    d = h_srcs[0].shape[1]
    d_ff = w_out.shape[0]
    tm = rows.tile
    row_spec = pl.BlockSpec((tm, d), lambda i: (i, 0))
    out_rows = n_tiles * tm
    out_shape = [jax.ShapeDtypeStruct((out_rows, d), F32)]
    out_specs = [row_spec]
    if emit == "mixer_in":
        out_shape.append(jax.ShapeDtypeStruct((out_rows, d), BF16))
        out_specs.append(row_spec)
    in_specs = (rows.source_specs(len(h_srcs), d)
                + [pl.BlockSpec((1, N_MOD, d), lambda i: (rows.mod_row(i), 0, 0)),
                   _resident(nw.shape), _resident(w_in.shape), _resident(w_out.shape),
                   _resident(fin_w.shape)])
    args = list(h_srcs) + [mod, nw, w_in, w_out, fin_w]
    if a_srcs:
        in_specs += rows.source_specs(len(a_srcs), a_srcs[0].shape[1]) + [_resident(w_proj.shape)]
        args += list(a_srcs) + [w_proj]
    return pl.pallas_call(
        functools.partial(_ffn_kernel, sub=sub, d_ff=d_ff, emit=emit, n_h=len(h_srcs),
                          n_a=len(a_srcs), lat_tiles=rows.lat_tiles),
        grid=(n_tiles,),
        in_specs=in_specs,
        out_specs=out_specs,
        out_shape=out_shape,
        compiler_params=_cparams(1),
        name=f"half_ffn_{emit}_{len(h_srcs)}h{len(a_srcs)}a",
    )(*args)


def _rope_tables(seq, pad_rows):
    pos = jnp.arange(seq)
    row_pos = (pos // GRID_W).astype(F32)
    col_pos = (pos % GRID_W).astype(F32)
    half = ATTN_HEAD_DIM // 2
    inv_freq = 1.0 / (ROPE_THETA ** (jnp.arange(0, half, 2, dtype=F32) / half))
    ang = jnp.stack([row_pos[:, None] * inv_freq, col_pos[:, None] * inv_freq], axis=1)
    cos = jnp.cos(ang)
    sin = jnp.sin(ang)
    cos64 = jnp.stack([cos, cos], axis=2).reshape(seq, ATTN_HEAD_DIM)
    sin64 = jnp.stack([-sin, sin], axis=2).reshape(seq, ATTN_HEAD_DIM)
    cos128 = jnp.concatenate([cos64, cos64], axis=1)
    sin128 = jnp.concatenate([sin64, sin64], axis=1)
    cos128 = jnp.concatenate([cos128, jnp.ones((pad_rows, LANES), F32)], axis=0)
    sin128 = jnp.concatenate([sin128, jnp.zeros((pad_rows, LANES), F32)], axis=0)
    return cos128, sin128


def _rope(x, cos, sin, first_half):
    fwd = pltpu.roll(x, LANES - 16, axis=1)
    bwd = pltpu.roll(x, 16, axis=1)
    return x * cos + jnp.where(first_half, fwd, bwd) * sin


def _qkv_kernel(u_ref, w_ref, cos_ref, sin_ref, q_ref, k_ref, v_ref):
    u = u_ref[...]
    d = u.shape[1]
    cos = cos_ref[...]
    sin = sin_ref[...]
    lane = lax.broadcasted_iota(jnp.int32, cos.shape, 1)
    first_half = (lane % 32) < 16
    q = _dot(u, w_ref[:, 0:d])
    k = _dot(u, w_ref[:, d:2 * d])
    v = _dot(u, w_ref[:, 2 * d:3 * d]).astype(BF16)
    qk_scale = ATTN_HEAD_DIM ** -0.5 * math.log2(math.e)
    ones = jnp.ones((u.shape[0], LANES), BF16)
    for hd in range(d // LANES):
        sl = slice(hd * LANES, (hd + 1) * LANES)
        q_ref[:, sl] = (_rope(q[:, sl], cos, sin, first_half) * qk_scale).astype(BF16)
        k_ref[:, sl] = _rope(k[:, sl], cos, sin, first_half).astype(BF16)
        v_ref[:, 2 * hd * LANES:(2 * hd + 1) * LANES] = v[:, sl]
        v_ref[:, (2 * hd + 1) * LANES:(2 * hd + 2) * LANES] = ones


def _attn_qkv(rows, u, w_qkv, cos, sin):
    d = u.shape[1]
    tm = rows.tile
    row_spec = pl.BlockSpec((tm, d), lambda i: (i, 0))
    tab_spec = pl.BlockSpec(
        (tm, LANES),
        lambda i: (jnp.where(i < rows.lat_tiles, i % rows.tiles_per_seq, rows.tiles_per_seq), 0))
    return pl.pallas_call(
        _qkv_kernel,
        grid=(rows.tiles,),
        in_specs=[row_spec, _resident(w_qkv.shape), tab_spec, tab_spec],
        out_specs=[row_spec, row_spec, pl.BlockSpec((tm, 2 * d), lambda i: (i, 0))],
        out_shape=[jax.ShapeDtypeStruct(u.shape, BF16), jax.ShapeDtypeStruct(u.shape, BF16),
                   jax.ShapeDtypeStruct((u.shape[0], 2 * d), BF16)],
        compiler_params=_cparams(1),
        name="attn_qkv",
    )(u, w_qkv, cos, sin)


def _attn_kernel(lam_ref, subln_ref, q_ref, *refs, n_seg, lambda_init, rows_per_chain):
    k_refs = refs[0:n_seg]
    v_refs = refs[n_seg:2 * n_seg]
    o_ref = refs[2 * n_seg]
    lv = lam_ref[...]
    lam = (jnp.exp(jnp.sum(lv[0:1] * lv[1:2], axis=-1, keepdims=True))
           - jnp.exp(jnp.sum(lv[2:3] * lv[3:4], axis=-1, keepdims=True)) + lambda_init)
    tq = q_ref.shape[0]
    for r0 in range(0, tq, rows_per_chain):
        q = q_ref[r0:r0 + rows_per_chain, :]
        lane = lax.broadcasted_iota(jnp.int32, q.shape, 1)
        zero = jnp.zeros_like(q)
        q_maps = (jnp.where(lane < ATTN_HEAD_DIM, q, zero), jnp.where(lane >= ATTN_HEAD_DIM, q, zero))
        outs = []
        for qm in q_maps:
            s = [_dot_nt(qm, k_ref[...]) for k_ref in k_refs]
            m = functools.reduce(jnp.maximum, [jnp.max(x, axis=-1, keepdims=True) for x in s])
            acc = None
            for x, v_ref in zip(s, v_refs):
                pv = _dot(jnp.exp2(x - m).astype(BF16), v_ref[...])
                acc = pv if acc is None else acc + pv
            outs.append(acc[:, 0:ATTN_V_DIM] / acc[:, ATTN_V_DIM:2 * ATTN_V_DIM])
        o = outs[0] - lam * outs[1]
        o_ref[r0:r0 + rows_per_chain, :] = (
            _rms(o, subln_ref[...], SUBLN_EPS) * (1.0 - lambda_init)).astype(BF16)


def _attention(q, k, v, lam, subln_w, *, batch, q_len, q_row0, segs, tq, lambda_init):
    d = q.shape[1]
    n_seg = len(segs)
    q_blk0 = q_row0 // tq
    q_tiles = q_len // tq

    def q_map(b, h, t):
        return (q_blk0 + b * q_tiles + t, h)

    def o_map(b, h, t):
        return (b * q_tiles + t, h)

    k_specs, v_specs = [], []
    for first, length in segs:
        assert first % length == 0
        kv_map = lambda b, h, t, f=first // length: (f + b, h)
        k_specs.append(pl.BlockSpec((length, LANES), kv_map))
        v_specs.append(pl.BlockSpec((length, 2 * LANES), kv_map))
    return pl.pallas_call(
        functools.partial(_attn_kernel, n_seg=n_seg, lambda_init=lambda_init,
                          rows_per_chain=min(tq, ATTN_CHAIN_ROWS)),
        grid=(batch, ATTN_HEADS, q_tiles),
        in_specs=[pl.BlockSpec(lam.shape, lambda b, h, t: (0, 0)),
                  pl.BlockSpec(subln_w.shape, lambda b, h, t: (0, 0)),
                  pl.BlockSpec((tq, LANES), q_map)] + k_specs + v_specs,
        out_specs=pl.BlockSpec((tq, LANES), o_map),
        out_shape=jax.ShapeDtypeStruct((batch * q_len, d), BF16),
        compiler_params=_cparams(3),
        name=f"diff_attention_{n_seg}seg",
    )(lam, subln_w, q, *([k] * n_seg), *([v] * n_seg))


def _up_kernel(u_ref, w_ref, x_ref, z_ref):
    u = u_ref[...]
    inner = x_ref.shape[1]
    x_ref[...] = _dot(u, w_ref[:, 0:inner])
    z_ref[...] = _dot(u, w_ref[:, inner:2 * inner]).astype(BF16)


def _mlstm_up(rows, u, w_up):
    d = u.shape[1]
    inner = w_up.shape[1] // 2
    tm = rows.tile
    out_spec = pl.BlockSpec((tm, inner), lambda i: (i, 0))
    return pl.pallas_call(
        _up_kernel,
        grid=(rows.tiles,),
        in_specs=[pl.BlockSpec((tm, d), lambda i: (i, 0)), _resident(w_up.shape)],
        out_specs=[out_spec, out_spec],
        out_shape=[jax.ShapeDtypeStruct((rows.rows, inner), F32),
                   jax.ShapeDtypeStruct((rows.rows, inner), BF16)],
        compiler_params=_cparams(1),
        name="mlstm_up",
    )(u, w_up)


def _log_sigmoid(x):
    return jnp.minimum(x, 0.0) - jnp.log1p(jnp.exp(-jnp.abs(x)))


def _feat_kernel(x_ref, xp_ref, xn_ref, cw_ref, cb_ref, wq_ref, wk_ref, wv_ref, wg_ref, bg_ref,
                 q_ref, k_ref, v_ref, xc_ref, g_ref, gt_ref, xe_ref,
                 *, tf, lat_tiles, lat_tps, ctx_tps, k_scale):
    i = pl.program_id(0)
    j = jnp.where(i < lat_tiles, i % lat_tps, (i - lat_tiles) % ctx_tps)
    tps = jnp.where(i < lat_tiles, lat_tps, ctx_tps)
    first = j == 0
    last = j == tps - 1
    x = x_ref[...]
    halo = SUBLANES
    xe_ref[0:halo, :] = jnp.where(first, 0.0, xp_ref[...])
    xe_ref[halo:halo + tf, :] = x
    xe_ref[halo + tf:2 * halo + tf, :] = jnp.where(last, 0.0, xn_ref[...])
    conv = jnp.zeros(x.shape, F32) + cb_ref[...]
    for t in range(CONV_W):
        off = halo - CONV_W // 2 + t
        conv = conv + cw_ref[t:t + 1, :] * xe_ref[off:off + tf, :]
    xc = _silu(conv)
    xc_ref[...] = xc.astype(BF16)
    xc_b = xc.astype(BF16)
    x_b = x.astype(BF16)
    nblk = wq_ref.shape[0]
    bw = wq_ref.shape[1]
    inner = x.shape[1]
    gates = jnp.zeros((tf, LANES), F32) + bg_ref[...]
    for part, (src, w_ref, o_ref, scl) in enumerate(
            ((xc_b, wq_ref, q_ref, 1.0), (xc_b, wk_ref, k_ref, k_scale), (x_b, wv_ref, v_ref, 1.0))):
        for jb in range(nblk):
            sl = slice(jb * bw, (jb + 1) * bw)
            y = _dot(src[:, sl], w_ref[jb])
            yb = y.astype(BF16)
            o_ref[:, sl] = yb if scl == 1.0 else (y * scl).astype(BF16)
            gates = gates + _dot(yb, wg_ref[part * inner + jb * bw:part * inner + (jb + 1) * bw, :])
    lane = lax.broadcasted_iota(jnp.int32, gates.shape, 1)
    is_forget = (lane % (2 * MLSTM_HEADS)) >= MLSTM_HEADS
    gates = jnp.where(is_forget, _log_sigmoid(gates), gates)
    g1 = pltpu.roll(gates, LANES - 2 * MLSTM_HEADS, axis=1)
    g_ref[0] = gates
    g_ref[1] = g1
    gt_ref[0] = gates.T[0:2 * SUBLANES, :]
    gt_ref[1] = g1.T[0:2 * SUBLANES, :]


def _mlstm_features(rows_feat, x_in, conv_w, conv_b, wq, wk, wv, wg, bg, *, k_scale):
    r, inner = x_in.shape
    tf = rows_feat.tile
    halo = SUBLANES
    nh = r // halo
    main = pl.BlockSpec((tf, inner), lambda i: (i, 0))
    prev = pl.BlockSpec((halo, inner), lambda i: (jnp.maximum(i * (tf // halo) - 1, 0), 0))
    nxt = pl.BlockSpec((halo, inner), lambda i: (jnp.minimum((i + 1) * (tf // halo), nh - 1), 0))
    bf_out = jax.ShapeDtypeStruct((r, inner), BF16)
    return pl.pallas_call(
        functools.partial(_feat_kernel, tf=tf, lat_tiles=rows_feat.lat_tiles,
                          lat_tps=rows_feat.tiles_per_seq, ctx_tps=rows_feat.ctx // tf,
                          k_scale=k_scale),
        grid=(rows_feat.tiles,),
        in_specs=[main, prev, nxt, _resident(conv_w.shape), _resident(conv_b.shape),
                  _resident(wq.shape), _resident(wk.shape), _resident(wv.shape),
                  _resident(wg.shape), _resident(bg.shape)],
        out_specs=[main, main, main, main,
                   pl.BlockSpec((2, tf, LANES), lambda i: (0, i, 0)),
                   pl.BlockSpec((2, 2 * SUBLANES, tf), lambda i: (0, 0, i))],
        out_shape=[bf_out, bf_out, bf_out, bf_out,
                   jax.ShapeDtypeStruct((2, r, LANES), F32),
                   jax.ShapeDtypeStruct((2, 2 * SUBLANES, r), F32)],
        scratch_shapes=[pltpu.VMEM((tf + 2 * halo, inner), F32)],
        compiler_params=_cparams(1),
        name="mlstm_features",
    )(x_in, x_in, x_in, conv_w, conv_b, wq, wk, wv, wg, bg)


def _split3(x):
    a = x.astype(BF16)
    r = x - a.astype(F32)
    b = r.astype(BF16)
    c = (r - b.astype(F32)).astype(BF16)
    return a, b, c


def _scan_kernel(q_ref, k_ref, v_ref, g_ref, gt_ref, h_ref, c_scr, cb_scr, n_scr, m_scr,
                 *, chunk, dh):
    d = pl.program_id(1)
    p = pl.program_id(2)

    @pl.when(p == 0)
    def _():
        c_scr[...] = jnp.zeros(c_scr.shape, F32)
        cb_scr[...] = jnp.zeros(cb_scr.shape, BF16)
        n_scr[...] = jnp.zeros(n_scr.shape, F32)
        m_scr[...] = jnp.zeros(m_scr.shape, F32)

    g = g_ref[0]
    gt = gt_ref[0]
    ti = lax.broadcasted_iota(jnp.int32, (chunk, chunk), 0)
    si = lax.broadcasted_iota(jnp.int32, (chunk, chunk), 1)
    sgn = 1 - 2 * d
    vis = (ti - si) * sgn >= 0
    tri = jnp.where(vis, 1.0, 0.0).astype(BF16)
    tri_t = jnp.where((si - ti) * sgn >= 0, 1.0, 0.0).astype(BF16)
    bc_all = functools.reduce(lambda a, b: a + b, [_dot(tri, t) for t in _split3(g)])
    br_all = functools.reduce(lambda a, b: a + b, [_dot(t, tri_t) for t in _split3(gt)])
    neg_inf = jnp.float32(-jnp.inf)
    for hd in range(MLSTM_HEADS):
        sl = slice(hd * dh, (hd + 1) * dh)
        q = q_ref[:, sl]
        k = k_ref[:, sl]
        v = v_ref[:, sl]
        ic = g[:, hd:hd + 1]
        fc = g[:, MLSTM_HEADS + hd:MLSTM_HEADS + hd + 1]
        bc = bc_all[:, MLSTM_HEADS + hd:MLSTM_HEADS + hd + 1]
        ir = gt[hd:hd + 1, :]
        br = br_all[MLSTM_HEADS + hd:MLSTM_HEADS + hd + 1, :]
        m_prev = m_scr[hd]
        n_vec = n_scr[hd]
        log_d = jnp.where(vis, bc - br + ir, neg_inf)
        m_inter = bc + m_prev
        m_t = jnp.maximum(m_inter, jnp.max(log_d, axis=-1, keepdims=True))
        s = _dot_nt(q, k) * jnp.exp(log_d - m_t)
        w_inter = jnp.exp(m_inter - m_t)
        qf = q.astype(F32)
        num = _dot(s.astype(BF16), v) + w_inter * _dot_nt(q, cb_scr[hd])
        den = jnp.sum(s, axis=-1, keepdims=True) + w_inter * jnp.sum(qf * n_vec, axis=-1, keepdims=True)
        h_ref[0, :, sl] = (num / jnp.maximum(jnp.abs(den), jnp.exp(-m_t))).astype(h_ref.dtype)
        b_tot = jnp.sum(fc, axis=0, keepdims=True)
        log_w = b_tot - bc + ic
        m_new = jnp.maximum(b_tot + m_prev, jnp.max(log_w, axis=0, keepdims=True))
        w_s = jnp.exp(log_w - m_new)
        decay = jnp.exp(b_tot + m_prev - m_new)
        vw = (v.astype(F32) * w_s).astype(BF16)
        c_new = decay * c_scr[hd] + _dot_tn(vw, k)
        c_scr[hd] = c_new
        cb_scr[hd] = c_new.astype(BF16)
        n_scr[hd] = decay * n_vec + jnp.sum(k.astype(F32) * w_s, axis=0, keepdims=True)
        m_scr[hd] = m_new


def _mlstm_scan(q, k, v, g, gt, *, batch, seq, ctx, chunk):
    r, inner = q.shape
    dh = inner // MLSTM_HEADS
    ncc = ctx // chunk
    ncl = seq // chunk
    lat_blocks = batch * ncl

    def blk(b, d, p):
        jc = jnp.where(d == 0, p, ncc - 1 - p)
        jl = jnp.where(d == 0, p - ncc, ncl - 1 - (p - ncc))
        return jnp.where(p < ncc, lat_blocks + b * ncc + jc, b * ncl + jl)

    row_spec = pl.BlockSpec((chunk, inner), lambda b, d, p: (blk(b, d, p), 0))
    return pl.pallas_call(
        functools.partial(_scan_kernel, chunk=chunk, dh=dh),
        grid=(batch, 2, ncc + ncl),
        in_specs=[row_spec, row_spec, row_spec,
                  pl.BlockSpec((1, chunk, LANES), lambda b, d, p: (d, blk(b, d, p), 0)),
                  pl.BlockSpec((1, 2 * SUBLANES, chunk), lambda b, d, p: (d, 0, blk(b, d, p)))],
        out_specs=pl.BlockSpec((1, chunk, inner), lambda b, d, p: (d, blk(b, d, p), 0)),
        out_shape=jax.ShapeDtypeStruct((2, r, inner), BF16),
        scratch_shapes=[pltpu.VMEM((MLSTM_HEADS, dh, dh), F32),
                        pltpu.VMEM((MLSTM_HEADS, dh, dh), BF16),
                        pltpu.VMEM((MLSTM_HEADS, 1, dh), F32),
                        pltpu.VMEM((MLSTM_HEADS, 1, 1), F32)],
        compiler_params=_cparams(3),
        name="mlstm_scan",
    )(q, k, v, g, gt)


def _mlstm_out_kernel(hd_ref, xc_ref, z_ref, skip_ref, nw_ref, a_ref, *, dh):
    for hd in range(MLSTM_HEADS):
        sl = slice(hd * dh, (hd + 1) * dh)
        hh = hd_ref[0, :, sl].astype(F32) + hd_ref[1, :, sl].astype(F32)
        mu = jnp.mean(hh, axis=-1, keepdims=True)
        cen = hh - mu
        var = jnp.mean(cen * cen, axis=-1, keepdims=True)
        hn = cen * lax.rsqrt(var + HEAD_LN_EPS) * nw_ref[:, sl]
        y = (hn + skip_ref[:, sl] * xc_ref[:, sl].astype(F32)) * _silu(z_ref[:, sl].astype(F32))
        a_ref[:, sl] = y.astype(BF16)


def _mlstm_out(rows, n_tiles, hdir, xc, z, skip, norm_w):
    inner = xc.shape[1]
    tm = rows.tile
    row_spec = pl.BlockSpec((tm, inner), lambda i: (i, 0))
    return pl.pallas_call(
        functools.partial(_mlstm_out_kernel, dh=inner // MLSTM_HEADS),
        grid=(n_tiles,),
        in_specs=[pl.BlockSpec((2, tm, inner), lambda i: (0, i, 0)), row_spec, row_spec,
                  _resident(skip.shape), _resident(norm_w.shape)],
        out_specs=row_spec,
        out_shape=jax.ShapeDtypeStruct((n_tiles * tm, inner), BF16),
        compiler_params=_cparams(1),
        name="mlstm_out",
    )(hdir, xc, z, skip, norm_w)


def _blockdiag_dense(w, block):
    g = w.shape[0]
    per = block // QKV_BLOCK
    wt = w.reshape(g // per, per, QKV_BLOCK, QKV_BLOCK)
    eye = jnp.eye(per, dtype=w.dtype)
    dense = jnp.einsum("npio,pq->npiqo", wt, eye)
    return dense.reshape(g // per, block, block)


def _pick_tile(seq, ctx_rows, target):
    t = target
    while seq % t or ctx_rows % t:
        t //= 2
    return t


def kernel(x, c, ctx, c_ctx, w_mod, b_mod, norm_w, ffn_w_in, ffn_w_out,
           attn_w_qkv, attn_w_o, attn_lambda, attn_subln_w,
           mlstm_w_up, mlstm_conv_w, mlstm_conv_b, mlstm_w_qkv, mlstm_w_gates, mlstm_b_gates,
           mlstm_skip, mlstm_norm_w, mlstm_w_down, final_norm_w):
    batch, seq, d = x.shape
    ctx_len = ctx.shape[1]
    depth = w_mod.shape[0]
    inner = mlstm_w_down.shape[1]
    dh = inner // MLSTM_HEADS

    rows = _Rows(batch, seq, ctx_len, _pick_tile(seq, batch * ctx_len, 512))
    rows_feat = _Rows(batch, seq, ctx_len, _pick_tile(seq, ctx_len, 256))
    chunk = _pick_tile(seq, ctx_len, 256)
    tq = _pick_tile(seq, seq, 2048)

    n_mod_rows = -(-(batch + 1) // MOD_ROWS_PAD) * MOD_ROWS_PAD
    cc = jnp.concatenate([c, c_ctx[None, :], jnp.zeros((n_mod_rows - batch - 1, d), F32)], axis=0)
    mod_all = _modulation(cc, w_mod, b_mod).reshape(depth, n_mod_rows, N_MOD, d)

    cos, sin = _rope_tables(seq, rows.tile)
    fin_w = final_norm_w.reshape(1, d)
    h_srcs = (x.reshape(batch * seq, d), ctx.reshape(batch * ctx_len, d))

    for i in range(depth):
        last = i == depth - 1
        j = i // N_MIXERS
        mod = mod_all[i]
        nw = norm_w[i]
        h, u = _half_ffn(rows, rows.tiles, h_srcs, mod, nw,
                         ffn_w_in[i, 0].astype(BF16), ffn_w_out[i, 0].astype(BF16),
                         fin_w, sub=0, emit="mixer_in")
        n_post = rows.lat_tiles if last else rows.tiles
        if i % N_MIXERS == 0:
            lambda_init = 0.8 - 0.6 * math.exp(-0.3 * i)
            q, k, v = _attn_qkv(rows, u, attn_w_qkv[j].astype(BF16), cos, sin)
            lam = attn_lambda[j]
            subln = attn_subln_w[j].reshape(1, ATTN_V_DIM)
            o_lat = _attention(q, k, v, lam, subln, batch=batch, q_len=seq, q_row0=0,
                               segs=[(rows.n_lat, ctx_len), (0, seq)], tq=tq,
                               lambda_init=lambda_init)
            o_ctx = _attention(q, k, v, lam, subln, batch=batch, q_len=ctx_len, q_row0=rows.n_lat,
                               segs=[(rows.n_lat, ctx_len)], tq=ctx_len, lambda_init=lambda_init)
            a_srcs = (o_lat, o_ctx)
            w_proj = attn_w_o[j].astype(BF16)
        else:
            x_in, z = _mlstm_up(rows, u, mlstm_w_up[j].astype(BF16))
            wq = _blockdiag_dense(mlstm_w_qkv[j, 0], MXU_DIM).astype(BF16)
            wk = _blockdiag_dense(mlstm_w_qkv[j, 1], MXU_DIM).astype(BF16)
            wv = _blockdiag_dense(mlstm_w_qkv[j, 2], MXU_DIM).astype(BF16)
            n_gate = mlstm_w_gates.shape[2]
            wg = jnp.pad(mlstm_w_gates[j], ((0, 0), (0, LANES - n_gate))).astype(BF16)
            bg = jnp.pad(mlstm_b_gates[j], (0, LANES - n_gate)).reshape(1, LANES)
            q, k, v, xc, g, gt = _mlstm_features(
                rows_feat, x_in, mlstm_conv_w[j], mlstm_conv_b[j].reshape(1, inner),
                wq, wk, wv, wg, bg, k_scale=dh ** -0.5)
            hdir = _mlstm_scan(q, k, v, g, gt, batch=batch, seq=seq, ctx=ctx_len, chunk=chunk)
            a = _mlstm_out(rows, n_post, hdir, xc, z, mlstm_skip[j].reshape(1, inner),
                           mlstm_norm_w[j].reshape(1, inner))
            a_srcs = (a,)
            w_proj = mlstm_w_down[j].astype(BF16)
        (h,) = _half_ffn(rows, n_post, (h,), mod, nw,
                         ffn_w_in[i, 1].astype(BF16), ffn_w_out[i, 1].astype(BF16),
                         fin_w, sub=2, emit="final" if last else "plain",
                         a_srcs=a_srcs, w_proj=w_proj)
        h_srcs = (h,)
    return h.reshape(batch, seq, d)
```

```python
import functools
import math

import jax
import jax.numpy as jnp
from jax import lax
from jax.experimental import pallas as pl
from jax.experimental.pallas import tpu as pltpu

N_MOD = 9
RMS_EPS = 1e-6
ATTN_HEADS = 8
ATTN_HEAD_DIM = 64
ATTN_V_DIM = 128
ROPE_THETA = 10000.0
GRID_W = 64
SUBLN_EPS = 1e-5
MLSTM_HEADS = 4
QKV_BLOCK = 4
CONV_W = 5
HEAD_LN_EPS = 1e-5
N_MIXERS = 2

LANES = 128
SUBLANES = 8
MXU_DIM = 256
MOD_ROWS_PAD = 8
ATTN_CHAIN_ROWS = 128
VMEM_LIMIT = 56 * 1024 * 1024

BF16 = jnp.bfloat16
F32 = jnp.float32


def _cparams(n_axes):
    return pltpu.CompilerParams(
        dimension_semantics=("arbitrary",) * n_axes,
        vmem_limit_bytes=VMEM_LIMIT)


def _resident(shape):
    nd = len(shape)
    return pl.BlockSpec(shape, lambda *_: (0,) * nd, pipeline_mode=pl.Buffered(1))


def _dot(a, b):
    return jnp.dot(a, b, preferred_element_type=F32)


def _dot_nt(a, b):
    return lax.dot_general(a, b, (((1,), (1,)), ((), ())), preferred_element_type=F32)


def _dot_tn(a, b):
    return lax.dot_general(a, b, (((0,), (0,)), ((), ())), preferred_element_type=F32)


def _sigmoid(x):
    return 1.0 / (1.0 + jnp.exp(-x))


def _silu(x):
    return x * _sigmoid(x)


class _Rows:
    def __init__(self, batch, seq, ctx, tile):
        assert seq % tile == 0 and (batch * ctx) % tile == 0
        self.batch, self.seq, self.ctx, self.tile = batch, seq, ctx, tile
        self.n_lat = batch * seq
        self.rows = batch * (seq + ctx)
        self.lat_tiles = self.n_lat // tile
        self.tiles = self.rows // tile
        self.tiles_per_seq = seq // tile

    def mod_row(self, i):
        return jnp.where(i < self.lat_tiles, i // self.tiles_per_seq, self.batch)

    def source_specs(self, n_src, width):
        tm = self.tile
        if n_src == 1:
            return [pl.BlockSpec((tm, width), lambda i: (i, 0))]
        lat_last = self.lat_tiles - 1
        return [pl.BlockSpec((tm, width), lambda i: (jnp.minimum(i, lat_last), 0)),
                pl.BlockSpec((tm, width), lambda i: (jnp.maximum(i - self.lat_tiles, 0), 0))]


def _pick_source(refs, lat_tiles):
    if len(refs) == 1:
        return refs[0][...]
    return jnp.where(pl.program_id(0) < lat_tiles, refs[0][...], refs[1][...])


def _mod_kernel(c_ref, w_ref, b_ref, o_ref):
    s = _silu(c_ref[...])
    o_ref[0] = jnp.dot(s, w_ref[0], preferred_element_type=F32,
                       precision=lax.Precision.HIGHEST) + b_ref[0]


def _modulation(cc, w_mod, b_mod):
    depth, d, nd = w_mod.shape
    rows = cc.shape[0]
    tn = d
    return pl.pallas_call(
        _mod_kernel,
        grid=(depth, nd // tn),
        in_specs=[pl.BlockSpec((rows, d), lambda l, j: (0, 0)),
                  pl.BlockSpec((1, d, tn), lambda l, j: (l, 0, j)),
                  pl.BlockSpec((1, 1, tn), lambda l, j: (l, 0, j))],
        out_specs=pl.BlockSpec((1, rows, tn), lambda l, j: (l, 0, j)),
        out_shape=jax.ShapeDtypeStruct((depth, rows, nd), F32),
        compiler_params=_cparams(2),
        name="modulation",
    )(cc, w_mod, b_mod.reshape(depth, 1, nd))


def _rms(x, w, eps):
    return x * lax.rsqrt(jnp.mean(x * x, axis=-1, keepdims=True) + eps) * w


def _ffn_chunks(d_ff):
    step = 3 * MXU_DIM
    out, c = [], 0
    while c < d_ff:
        w = min(step, d_ff - c)
        out.append((c, w))
        c += w
    return out


def _ffn_kernel(*refs, sub, d_ff, emit, n_h, n_a, lat_tiles):
    h_refs = refs[:n_h]
    mod_ref, nw_ref, win_ref, wout_ref, fin_ref = refs[n_h:n_h + 5]
    pos = n_h + 5
    a_refs = refs[pos:pos + n_a]
    pos += n_a
    if n_a:
        wp_ref = refs[pos]
        pos += 1
    out_refs = refs[pos:]

    x = _pick_source(h_refs, lat_tiles)
    if n_a:
        x = x + mod_ref[0, 5:6, :] * _dot(_pick_source(a_refs, lat_tiles), wp_ref[...])
    shift = mod_ref[0, 3 * sub:3 * sub + 1, :]
    scale = mod_ref[0, 3 * sub + 1:3 * sub + 2, :]
    gate = mod_ref[0, 3 * sub + 2:3 * sub + 3, :]
    u = (_rms(x, nw_ref[sub:sub + 1, :], RMS_EPS) * (1.0 + scale) + shift).astype(BF16)
    acc = jnp.zeros(x.shape, F32)
    for c0, cw in _ffn_chunks(d_ff):
        g = _dot(u, win_ref[:, c0:c0 + cw])
        up = _dot(u, win_ref[:, d_ff + c0:d_ff + c0 + cw])
        hid = (_silu(g) * up).astype(BF16)
        acc = acc + _dot(hid, wout_ref[c0:c0 + cw, :])
    y = x + 0.5 * gate * acc
    if emit == "mixer_in":
        out_refs[0][...] = y
        u2 = _rms(y, nw_ref[1:2, :], RMS_EPS) * (1.0 + mod_ref[0, 4:5, :]) + mod_ref[0, 3:4, :]
        out_refs[1][...] = u2.astype(BF16)
    elif emit == "final":
        out_refs[0][...] = _rms(y, fin_ref[...], RMS_EPS)
    else:
        out_refs[0][...] = y


def _stacked_weight(w, index):
    lead = len(index)
    zeros = (0,) * (w.ndim - lead)
    return pl.BlockSpec((None,) * lead + w.shape[lead:], lambda *_: tuple(index) + zeros,
                        pipeline_mode=pl.Buffered(1))


def _half_ffn(rows, n_tiles, h_srcs, mod, nw, w_in, w_out, w_index, fin_w, *, sub, emit,
              a_srcs=(), w_proj=None):
    d = h_srcs[0].shape[1]
    d_ff = w_out.shape[-2]
    tm = rows.tile
    row_spec = pl.BlockSpec((tm, d), lambda i: (i, 0))
    out_rows = n_tiles * tm
    out_shape = [jax.ShapeDtypeStruct((out_rows, d), F32)]
    out_specs = [row_spec]
    if emit == "mixer_in":
        out_shape.append(jax.ShapeDtypeStruct((out_rows, d), BF16))
        out_specs.append(row_spec)
    in_specs = (rows.source_specs(len(h_srcs), d)
                + [pl.BlockSpec((1, N_MOD, d), lambda i: (rows.mod_row(i), 0, 0)),
                   _resident(nw.shape), _stacked_weight(w_in, w_index),
                   _stacked_weight(w_out, w_index),
                   _resident(fin_w.shape)])
    args = list(h_srcs) + [mod, nw, w_in, w_out, fin_w]
    if a_srcs:
        in_specs += rows.source_specs(len(a_srcs), a_srcs[0].shape[1]) + [_resident(w_proj.shape)]
        args += list(a_srcs) + [w_proj]
    return pl.pallas_call(
        functools.partial(_ffn_kernel, sub=sub, d_ff=d_ff, emit=emit, n_h=len(h_srcs),
                          n_a=len(a_srcs), lat_tiles=rows.lat_tiles),
        grid=(n_tiles,),
        in_specs=in_specs,
        out_specs=out_specs,
        out_shape=out_shape,
        compiler_params=_cparams(1),
        name=f"half_ffn_{emit}_{len(h_srcs)}h{len(a_srcs)}a",
    )(*args)


def _rope_tables(seq, pad_rows):
    pos = jnp.arange(seq)
    row_pos = (pos // GRID_W).astype(F32)
    col_pos = (pos % GRID_W).astype(F32)
    half = ATTN_HEAD_DIM // 2
    inv_freq = 1.0 / (ROPE_THETA ** (jnp.arange(0, half, 2, dtype=F32) / half))
    ang = jnp.stack([row_pos[:, None] * inv_freq, col_pos[:, None] * inv_freq], axis=1)
    cos = jnp.cos(ang)
    sin = jnp.sin(ang)
    cos64 = jnp.stack([cos, cos], axis=2).reshape(seq, ATTN_HEAD_DIM)
    sin64 = jnp.stack([-sin, sin], axis=2).reshape(seq, ATTN_HEAD_DIM)
    cos128 = jnp.concatenate([cos64, cos64], axis=1)
    sin128 = jnp.concatenate([sin64, sin64], axis=1)
    cos128 = jnp.concatenate([cos128, jnp.ones((pad_rows, LANES), F32)], axis=0)
    sin128 = jnp.concatenate([sin128, jnp.zeros((pad_rows, LANES), F32)], axis=0)
    return cos128, sin128


def _rope(x, cos, sin, first_half):
    fwd = pltpu.roll(x, LANES - 16, axis=1)
    bwd = pltpu.roll(x, 16, axis=1)
    return x * cos + jnp.where(first_half, fwd, bwd) * sin


def _qkv_kernel(u_ref, w_ref, cos_ref, sin_ref, q_ref, k_ref, v_ref):
    u = u_ref[...]
    d = u.shape[1]
    cos = cos_ref[...]
    sin = sin_ref[...]
    lane = lax.broadcasted_iota(jnp.int32, cos.shape, 1)
    first_half = (lane % 32) < 16
    q = _dot(u, w_ref[:, 0:d])
    k = _dot(u, w_ref[:, d:2 * d])
    v = _dot(u, w_ref[:, 2 * d:3 * d]).astype(BF16)
    qk_scale = ATTN_HEAD_DIM ** -0.5 * math.log2(math.e)
    ones = jnp.ones((u.shape[0], LANES), BF16)
    for hd in range(d // LANES):
        sl = slice(hd * LANES, (hd + 1) * LANES)
        q_ref[:, sl] = (_rope(q[:, sl], cos, sin, first_half) * qk_scale).astype(BF16)
        k_ref[:, sl] = _rope(k[:, sl], cos, sin, first_half).astype(BF16)
        v_ref[:, 2 * hd * LANES:(2 * hd + 1) * LANES] = v[:, sl]
        v_ref[:, (2 * hd + 1) * LANES:(2 * hd + 2) * LANES] = ones


def _attn_qkv(rows, u, w_qkv, cos, sin):
    d = u.shape[1]
    tm = rows.tile
    row_spec = pl.BlockSpec((tm, d), lambda i: (i, 0))
    tab_spec = pl.BlockSpec(
        (tm, LANES),
        lambda i: (jnp.where(i < rows.lat_tiles, i % rows.tiles_per_seq, rows.tiles_per_seq), 0))
    return pl.pallas_call(
        _qkv_kernel,
        grid=(rows.tiles,),
        in_specs=[row_spec, _resident(w_qkv.shape), tab_spec, tab_spec],
        out_specs=[row_spec, row_spec, pl.BlockSpec((tm, 2 * d), lambda i: (i, 0))],
        out_shape=[jax.ShapeDtypeStruct(u.shape, BF16), jax.ShapeDtypeStruct(u.shape, BF16),
                   jax.ShapeDtypeStruct((u.shape[0], 2 * d), BF16)],
        compiler_params=_cparams(1),
        name="attn_qkv",
    )(u, w_qkv, cos, sin)


def _attn_kernel(lam_ref, subln_ref, q_ref, *refs, n_seg, lambda_init, rows_per_chain):
    k_refs = refs[0:n_seg]
    v_refs = refs[n_seg:2 * n_seg]
    o_ref = refs[2 * n_seg]
    lv = lam_ref[...]
    lam = (jnp.exp(jnp.sum(lv[0:1] * lv[1:2], axis=-1, keepdims=True))
           - jnp.exp(jnp.sum(lv[2:3] * lv[3:4], axis=-1, keepdims=True)) + lambda_init)
    tq = q_ref.shape[0]
    for r0 in range(0, tq, rows_per_chain):
        q = q_ref[r0:r0 + rows_per_chain, :]
        lane = lax.broadcasted_iota(jnp.int32, q.shape, 1)
        zero = jnp.zeros_like(q)
        q_maps = (jnp.where(lane < ATTN_HEAD_DIM, q, zero), jnp.where(lane >= ATTN_HEAD_DIM, q, zero))
        outs = []
        for qm in q_maps:
            s = [_dot_nt(qm, k_ref[...]) for k_ref in k_refs]
            m = functools.reduce(jnp.maximum, [jnp.max(x, axis=-1, keepdims=True) for x in s])
            acc = None
            for x, v_ref in zip(s, v_refs):
                pv = _dot(jnp.exp2(x - m).astype(BF16), v_ref[...])
                acc = pv if acc is None else acc + pv
            outs.append(acc[:, 0:ATTN_V_DIM] / acc[:, ATTN_V_DIM:2 * ATTN_V_DIM])
        o = outs[0] - lam * outs[1]
        o_ref[r0:r0 + rows_per_chain, :] = (
            _rms(o, subln_ref[...], SUBLN_EPS) * (1.0 - lambda_init)).astype(BF16)


def _attention(q, k, v, lam, subln_w, *, batch, q_len, q_row0, segs, tq, lambda_init):
    d = q.shape[1]
    n_seg = len(segs)
    q_blk0 = q_row0 // tq
    q_tiles = q_len // tq

    def q_map(b, h, t):
        return (q_blk0 + b * q_tiles + t, h)

    def o_map(b, h, t):
        return (b * q_tiles + t, h)

    k_specs, v_specs = [], []
    for first, length in segs:
        assert first % length == 0
        kv_map = lambda b, h, t, f=first // length: (f + b, h)
        k_specs.append(pl.BlockSpec((length, LANES), kv_map))
        v_specs.append(pl.BlockSpec((length, 2 * LANES), kv_map))
    return pl.pallas_call(
        functools.partial(_attn_kernel, n_seg=n_seg, lambda_init=lambda_init,
                          rows_per_chain=min(tq, ATTN_CHAIN_ROWS)),
        grid=(batch, ATTN_HEADS, q_tiles),
        in_specs=[pl.BlockSpec(lam.shape, lambda b, h, t: (0, 0)),
                  pl.BlockSpec(subln_w.shape, lambda b, h, t: (0, 0)),
                  pl.BlockSpec((tq, LANES), q_map)] + k_specs + v_specs,
        out_specs=pl.BlockSpec((tq, LANES), o_map),
        out_shape=jax.ShapeDtypeStruct((batch * q_len, d), BF16),
        compiler_params=_cparams(3),
        name=f"diff_attention_{n_seg}seg",
    )(lam, subln_w, q, *([k] * n_seg), *([v] * n_seg))


def _log_sigmoid(x):
    return jnp.minimum(x, 0.0) - jnp.log1p(jnp.exp(-jnp.abs(x)))


HALO = 16


def _feat_kernel(u_ref, up_ref, un_ref, wup_ref, cw_ref, cb_ref, wq_ref, wk_ref, wv_ref, wg_ref, bg_ref,
                 q_ref, k_ref, v_ref, xc_ref, z_ref, g_ref, gt_ref, xe_ref,
                 *, tf, lat_tiles, lat_tps, ctx_tps, k_scale):
    i = pl.program_id(0)
    j = jnp.where(i < lat_tiles, i % lat_tps, (i - lat_tiles) % ctx_tps)
    tps = jnp.where(i < lat_tiles, lat_tps, ctx_tps)
    first = j == 0
    last = j == tps - 1
    halo = HALO
    inner = xc_ref.shape[1]
    u = u_ref[...]
    u_ext = jnp.concatenate([up_ref[...], u, un_ref[...]], axis=0)
    x_ext = _dot(u_ext, wup_ref[:, 0:inner])
    z_ref[...] = _dot(u, wup_ref[:, inner:2 * inner]).astype(BF16)
    x = x_ext[halo:halo + tf, :]
    xe_ref[0:halo, :] = jnp.where(first, 0.0, x_ext[0:halo, :])
    xe_ref[halo:halo + tf, :] = x
    xe_ref[halo + tf:2 * halo + tf, :] = jnp.where(last, 0.0, x_ext[halo + tf:2 * halo + tf, :])
    conv = jnp.zeros(x.shape, F32) + cb_ref[...]
    for t in range(CONV_W):
        off = halo - CONV_W // 2 + t
        conv = conv + cw_ref[t:t + 1, :] * xe_ref[off:off + tf, :]
    xc = _silu(conv)
    xc_ref[...] = xc.astype(BF16)
    xc_b = xc.astype(BF16)
    x_b = x.astype(BF16)
    nblk = wq_ref.shape[0]
    bw = wq_ref.shape[1]
    gates = jnp.zeros((tf, LANES), F32) + bg_ref[...]
    for part, (src, w_ref, o_ref, scl) in enumerate(
            ((xc_b, wq_ref, q_ref, 1.0), (xc_b, wk_ref, k_ref, k_scale), (x_b, wv_ref, v_ref, 1.0))):
        for jb in range(nblk):
            sl = slice(jb * bw, (jb + 1) * bw)
            y = _dot(src[:, sl], w_ref[jb])
            yb = y.astype(BF16)
            o_ref[:, sl] = yb if scl == 1.0 else (y * scl).astype(BF16)
            gates = gates + _dot(yb, wg_ref[part * inner + jb * bw:part * inner + (jb + 1) * bw, :])
    lane = lax.broadcasted_iota(jnp.int32, gates.shape, 1)
    is_forget = (lane % (2 * MLSTM_HEADS)) >= MLSTM_HEADS
    gates = jnp.where(is_forget, _log_sigmoid(gates), gates)
    g1 = pltpu.roll(gates, LANES - 2 * MLSTM_HEADS, axis=1)
    g_ref[0] = gates
    g_ref[1] = g1
    gt_ref[0] = gates.T[0:2 * SUBLANES, :]
    gt_ref[1] = g1.T[0:2 * SUBLANES, :]


def _mlstm_features(rows_feat, u, w_up, conv_w, conv_b, wq, wk, wv, wg, bg, *, k_scale):
    r, d = u.shape
    inner = w_up.shape[1] // 2
    tf = rows_feat.tile
    halo = HALO
    nh = r // halo
    u_main = pl.BlockSpec((tf, d), lambda i: (i, 0))
    u_prev = pl.BlockSpec((halo, d), lambda i: (jnp.maximum(i * (tf // halo) - 1, 0), 0))
    u_next = pl.BlockSpec((halo, d), lambda i: (jnp.minimum((i + 1) * (tf // halo), nh - 1), 0))
    main = pl.BlockSpec((tf, inner), lambda i: (i, 0))
    bf_out = jax.ShapeDtypeStruct((r, inner), BF16)
    return pl.pallas_call(
        functools.partial(_feat_kernel, tf=tf, lat_tiles=rows_feat.lat_tiles,
                          lat_tps=rows_feat.tiles_per_seq, ctx_tps=rows_feat.ctx // tf,
                          k_scale=k_scale),
        grid=(rows_feat.tiles,),
        in_specs=[u_main, u_prev, u_next, _resident(w_up.shape),
                  _resident(conv_w.shape), _resident(conv_b.shape),
                  _resident(wq.shape), _resident(wk.shape), _resident(wv.shape),
                  _resident(wg.shape), _resident(bg.shape)],
        out_specs=[main, main, main, main, main,
                   pl.BlockSpec((2, tf, LANES), lambda i: (0, i, 0)),
                   pl.BlockSpec((2, 2 * SUBLANES, tf), lambda i: (0, 0, i))],
        out_shape=[bf_out, bf_out, bf_out, bf_out, bf_out,
                   jax.ShapeDtypeStruct((2, r, LANES), F32),
                   jax.ShapeDtypeStruct((2, 2 * SUBLANES, r), F32)],
        scratch_shapes=[pltpu.VMEM((tf + 2 * halo, inner), F32)],
        compiler_params=_cparams(1),
        name="mlstm_features",
    )(u, u, u, w_up, conv_w, conv_b, wq, wk, wv, wg, bg)


def _split3(x):
    a = x.astype(BF16)
    r = x - a.astype(F32)
    b = r.astype(BF16)
    c = (r - b.astype(F32)).astype(BF16)
    return a, b, c


def _scan_kernel(q_ref, k_ref, v_ref, g_ref, gt_ref, h_ref, c_scr, cb_scr, n_scr, m_scr,
                 *, chunk, dh):
    d = pl.program_id(1)
    p = pl.program_id(2)

    @pl.when(p == 0)
    def _():
        c_scr[...] = jnp.zeros(c_scr.shape, F32)
        cb_scr[...] = jnp.zeros(cb_scr.shape, BF16)
        n_scr[...] = jnp.zeros(n_scr.shape, F32)
        m_scr[...] = jnp.zeros(m_scr.shape, F32)

    g = g_ref[0]
    gt = gt_ref[0]
    ti = lax.broadcasted_iota(jnp.int32, (chunk, chunk), 0)
    si = lax.broadcasted_iota(jnp.int32, (chunk, chunk), 1)
    sgn = 1 - 2 * d
    vis = (ti - si) * sgn >= 0
    tri = jnp.where(vis, 1.0, 0.0).astype(BF16)
    tri_t = jnp.where((si - ti) * sgn >= 0, 1.0, 0.0).astype(BF16)
    bc_all = functools.reduce(lambda a, b: a + b, [_dot(tri, t) for t in _split3(g)])
    br_all = functools.reduce(lambda a, b: a + b, [_dot(t, tri_t) for t in _split3(gt)])
    neg_inf = jnp.float32(-jnp.inf)
    for hd in range(MLSTM_HEADS):
        sl = slice(hd * dh, (hd + 1) * dh)
        q = q_ref[:, sl]
        k = k_ref[:, sl]
        v = v_ref[:, sl]
        ic = g[:, hd:hd + 1]
        fc = g[:, MLSTM_HEADS + hd:MLSTM_HEADS + hd + 1]
        bc = bc_all[:, MLSTM_HEADS + hd:MLSTM_HEADS + hd + 1]
        ir = gt[hd:hd + 1, :]
        br = br_all[MLSTM_HEADS + hd:MLSTM_HEADS + hd + 1, :]
        m_prev = m_scr[hd]
        n_vec = n_scr[hd]
        log_d = jnp.where(vis, bc - br + ir, neg_inf)
        m_inter = bc + m_prev
        m_t = jnp.maximum(m_inter, jnp.max(log_d, axis=-1, keepdims=True))
        s = _dot_nt(q, k) * jnp.exp(log_d - m_t)
        w_inter = jnp.exp(m_inter - m_t)
        qf = q.astype(F32)
        num = _dot(s.astype(BF16), v) + w_inter * _dot_nt(q, cb_scr[hd])
        den = jnp.sum(s, axis=-1, keepdims=True) + w_inter * jnp.sum(qf * n_vec, axis=-1, keepdims=True)
        h_ref[0, :, sl] = (num / jnp.maximum(jnp.abs(den), jnp.exp(-m_t))).astype(h_ref.dtype)
        b_tot = jnp.sum(fc, axis=0, keepdims=True)
        log_w = b_tot - bc + ic
        m_new = jnp.maximum(b_tot + m_prev, jnp.max(log_w, axis=0, keepdims=True))
        w_s = jnp.exp(log_w - m_new)
        decay = jnp.exp(b_tot + m_prev - m_new)
        vw = (v.astype(F32) * w_s).astype(BF16)
        c_new = decay * c_scr[hd] + _dot_tn(vw, k)
        c_scr[hd] = c_new
        cb_scr[hd] = c_new.astype(BF16)
        n_scr[hd] = decay * n_vec + jnp.sum(k.astype(F32) * w_s, axis=0, keepdims=True)
        m_scr[hd] = m_new


def _mlstm_scan(q, k, v, g, gt, *, batch, seq, ctx, chunk):
    r, inner = q.shape
    dh = inner // MLSTM_HEADS
    ncc = ctx // chunk
    ncl = seq // chunk
    lat_blocks = batch * ncl

    def blk(b, d, p):
        jc = jnp.where(d == 0, p, ncc - 1 - p)
        jl = jnp.where(d == 0, p - ncc, ncl - 1 - (p - ncc))
        return jnp.where(p < ncc, lat_blocks + b * ncc + jc, b * ncl + jl)

    row_spec = pl.BlockSpec((chunk, inner), lambda b, d, p: (blk(b, d, p), 0))
    return pl.pallas_call(
        functools.partial(_scan_kernel, chunk=chunk, dh=dh),
        grid=(batch, 2, ncc + ncl),
        in_specs=[row_spec, row_spec, row_spec,
                  pl.BlockSpec((1, chunk, LANES), lambda b, d, p: (d, blk(b, d, p), 0)),
                  pl.BlockSpec((1, 2 * SUBLANES, chunk), lambda b, d, p: (d, 0, blk(b, d, p)))],
        out_specs=pl.BlockSpec((1, chunk, inner), lambda b, d, p: (d, blk(b, d, p), 0)),
        out_shape=jax.ShapeDtypeStruct((2, r, inner), BF16),
        scratch_shapes=[pltpu.VMEM((MLSTM_HEADS, dh, dh), F32),
                        pltpu.VMEM((MLSTM_HEADS, dh, dh), BF16),
                        pltpu.VMEM((MLSTM_HEADS, 1, dh), F32),
                        pltpu.VMEM((MLSTM_HEADS, 1, 1), F32)],
        compiler_params=_cparams(3),
        name="mlstm_scan",
    )(q, k, v, g, gt)


def _mlstm_out_kernel(hd_ref, xc_ref, z_ref, skip_ref, nw_ref, a_ref, *, dh):
    for hd in range(MLSTM_HEADS):
        sl = slice(hd * dh, (hd + 1) * dh)
        hh = hd_ref[0, :, sl].astype(F32) + hd_ref[1, :, sl].astype(F32)
        mu = jnp.mean(hh, axis=-1, keepdims=True)
        cen = hh - mu
        var = jnp.mean(cen * cen, axis=-1, keepdims=True)
        hn = cen * lax.rsqrt(var + HEAD_LN_EPS) * nw_ref[:, sl]
        y = (hn + skip_ref[:, sl] * xc_ref[:, sl].astype(F32)) * _silu(z_ref[:, sl].astype(F32))
        a_ref[:, sl] = y.astype(BF16)


def _mlstm_out(rows, n_tiles, hdir, xc, z, skip, norm_w):
    inner = xc.shape[1]
    tm = rows.tile
    row_spec = pl.BlockSpec((tm, inner), lambda i: (i, 0))
    return pl.pallas_call(
        functools.partial(_mlstm_out_kernel, dh=inner // MLSTM_HEADS),
        grid=(n_tiles,),
        in_specs=[pl.BlockSpec((2, tm, inner), lambda i: (0, i, 0)), row_spec, row_spec,
                  _resident(skip.shape), _resident(norm_w.shape)],
        out_specs=row_spec,
        out_shape=jax.ShapeDtypeStruct((n_tiles * tm, inner), BF16),
        compiler_params=_cparams(1),
        name="mlstm_out",
    )(hdir, xc, z, skip, norm_w)


def _blockdiag_dense(w, block):
    g = w.shape[0]
    per = block // QKV_BLOCK
    wt = w.reshape(g // per, per, QKV_BLOCK, QKV_BLOCK)
    eye = jnp.eye(per, dtype=w.dtype)
    dense = jnp.einsum("npio,pq->npiqo", wt, eye)
    return dense.reshape(g // per, block, block)


def _pick_tile(seq, ctx_rows, target):
    t = target
    while seq % t or ctx_rows % t:
        t //= 2
    return t


def kernel(x, c, ctx, c_ctx, w_mod, b_mod, norm_w, ffn_w_in, ffn_w_out,
           attn_w_qkv, attn_w_o, attn_lambda, attn_subln_w,
           mlstm_w_up, mlstm_conv_w, mlstm_conv_b, mlstm_w_qkv, mlstm_w_gates, mlstm_b_gates,
           mlstm_skip, mlstm_norm_w, mlstm_w_down, final_norm_w):
    batch, seq, d = x.shape
    ctx_len = ctx.shape[1]
    depth = w_mod.shape[0]
    inner = mlstm_w_down.shape[1]
    dh = inner // MLSTM_HEADS

    rows = _Rows(batch, seq, ctx_len, _pick_tile(seq, batch * ctx_len, 512))
    rows_feat = _Rows(batch, seq, ctx_len, _pick_tile(seq, ctx_len, 256))
    chunk = _pick_tile(seq, ctx_len, 256)
    tq = _pick_tile(seq, seq, 2048)

    n_mod_rows = -(-(batch + 1) // MOD_ROWS_PAD) * MOD_ROWS_PAD
    cc = jnp.concatenate([c, c_ctx[None, :], jnp.zeros((n_mod_rows - batch - 1, d), F32)], axis=0)
    mod_all = _modulation(cc, w_mod, b_mod).reshape(depth, n_mod_rows, N_MOD, d)

    cos, sin = _rope_tables(seq, rows.tile)
    fin_w = final_norm_w.reshape(1, d)
    h_srcs = (x.reshape(batch * seq, d), ctx.reshape(batch * ctx_len, d))
    w_in_bf = ffn_w_in.astype(BF16)
    w_out_bf = ffn_w_out.astype(BF16)

    for i in range(depth):
        last = i == depth - 1
        j = i // N_MIXERS
        mod = mod_all[i]
        nw = norm_w[i]
        h, u = _half_ffn(rows, rows.tiles, h_srcs, mod, nw, w_in_bf, w_out_bf, (i, 0),
                         fin_w, sub=0, emit="mixer_in")
        n_post = rows.lat_tiles if last else rows.tiles
        if i % N_MIXERS == 0:
            lambda_init = 0.8 - 0.6 * math.exp(-0.3 * i)
            q, k, v = _attn_qkv(rows, u, attn_w_qkv[j].astype(BF16), cos, sin)
            lam = attn_lambda[j]
            subln = attn_subln_w[j].reshape(1, ATTN_V_DIM)
            o_lat = _attention(q, k, v, lam, subln, batch=batch, q_len=seq, q_row0=0,
                               segs=[(rows.n_lat, ctx_len), (0, seq)], tq=tq,
                               lambda_init=lambda_init)
            o_ctx = _attention(q, k, v, lam, subln, batch=batch, q_len=ctx_len, q_row0=rows.n_lat,
                               segs=[(rows.n_lat, ctx_len)], tq=ctx_len, lambda_init=lambda_init)
            a_srcs = (o_lat, o_ctx)
            w_proj = attn_w_o[j].astype(BF16)
        else:
            wq = _blockdiag_dense(mlstm_w_qkv[j, 0], MXU_DIM).astype(BF16)
            wk = _blockdiag_dense(mlstm_w_qkv[j, 1], MXU_DIM).astype(BF16)
            wv = _blockdiag_dense(mlstm_w_qkv[j, 2], MXU_DIM).astype(BF16)
            n_gate = mlstm_w_gates.shape[2]
            wg = jnp.pad(mlstm_w_gates[j], ((0, 0), (0, LANES - n_gate))).astype(BF16)
            bg = jnp.pad(mlstm_b_gates[j], (0, LANES - n_gate)).reshape(1, LANES)
            q, k, v, xc, z, g, gt = _mlstm_features(
                rows_feat, u, mlstm_w_up[j].astype(BF16),
                mlstm_conv_w[j], mlstm_conv_b[j].reshape(1, inner),
                wq, wk, wv, wg, bg, k_scale=dh ** -0.5)
            hdir = _mlstm_scan(q, k, v, g, gt, batch=batch, seq=seq, ctx=ctx_len, chunk=chunk)
            a = _mlstm_out(rows, n_post, hdir, xc, z, mlstm_skip[j].reshape(1, inner),
                           mlstm_norm_w[j].reshape(1, inner))
            a_srcs = (a,)
            w_proj = mlstm_w_down[j].astype(BF16)
        (h,) = _half_ffn(rows, n_post, (h,), mod, nw, w_in_bf, w_out_bf, (i, 1),
                         fin_w, sub=2, emit="final" if last else "plain",
                         a_srcs=a_srcs, w_proj=w_proj)
        h_srcs = (h,)
    return h.reshape(batch, seq, d)
```

```python
import functools
import math

import jax
import jax.numpy as jnp
from jax import lax
from jax.experimental import pallas as pl
from jax.experimental.pallas import tpu as pltpu

N_MOD = 9
RMS_EPS = 1e-6
ATTN_HEADS = 8
ATTN_HEAD_DIM = 64
ATTN_V_DIM = 128
ROPE_THETA = 10000.0
GRID_W = 64
SUBLN_EPS = 1e-5
MLSTM_HEADS = 4
QKV_BLOCK = 4
CONV_W = 5
HEAD_LN_EPS = 1e-5
N_MIXERS = 2

LANES = 128
SUBLANES = 8
MXU_DIM = 256
MOD_ROWS_PAD = 8
FFN_ROW_GROUPS = 2
ATTN_CHAIN_ROWS = 128
VMEM_LIMIT = 56 * 1024 * 1024

BF16 = jnp.bfloat16
F32 = jnp.float32


def _cparams(n_axes, **kwargs):
    return pltpu.CompilerParams(
        dimension_semantics=("arbitrary",) * n_axes,
        vmem_limit_bytes=VMEM_LIMIT, **kwargs)


def _resident(shape):
    nd = len(shape)
    return pl.BlockSpec(shape, lambda *_: (0,) * nd, pipeline_mode=pl.Buffered(1))


def _dot(a, b):
    return jnp.dot(a, b, preferred_element_type=F32)


def _dot_nt(a, b):
    return lax.dot_general(a, b, (((1,), (1,)), ((), ())), preferred_element_type=F32)


def _dot_tn(a, b):
    return lax.dot_general(a, b, (((0,), (0,)), ((), ())), preferred_element_type=F32)


def _sigmoid(x):
    return 1.0 / (1.0 + jnp.exp(-x))


def _silu(x):
    return x * _sigmoid(x)


class _Rows:
    def __init__(self, batch, seq, ctx, tile):
        assert seq % tile == 0 and (batch * ctx) % tile == 0
        self.batch, self.seq, self.ctx, self.tile = batch, seq, ctx, tile
        self.n_lat = batch * seq
        self.rows = batch * (seq + ctx)
        self.lat_tiles = self.n_lat // tile
        self.tiles = self.rows // tile
        self.tiles_per_seq = seq // tile

    def mod_row(self, i):
        return jnp.where(i < self.lat_tiles, i // self.tiles_per_seq, self.batch)

    def source_specs(self, n_src, width):
        tm = self.tile
        if n_src == 1:
            return [pl.BlockSpec((tm, width), lambda i: (i, 0))]
        lat_last = self.lat_tiles - 1
        return [pl.BlockSpec((tm, width), lambda i: (jnp.minimum(i, lat_last), 0)),
                pl.BlockSpec((tm, width), lambda i: (jnp.maximum(i - self.lat_tiles, 0), 0))]


def _pick_source(refs, lat_tiles, rows):
    if len(refs) == 1:
        return refs[0][rows, :]
    return jnp.where(pl.program_id(0) < lat_tiles, refs[0][rows, :], refs[1][rows, :])


def _mod_kernel(c_ref, w_ref, b_ref, o_ref):
    s = _silu(c_ref[...])
    o_ref[0] = jnp.dot(s, w_ref[0], preferred_element_type=F32,
                       precision=lax.Precision.HIGHEST) + b_ref[0]


def _modulation(cc, w_mod, b_mod):
    depth, d, nd = w_mod.shape
    rows = cc.shape[0]
    tn = d
    return pl.pallas_call(
        _mod_kernel,
        grid=(depth, nd // tn),
        in_specs=[pl.BlockSpec((rows, d), lambda l, j: (0, 0)),
                  pl.BlockSpec((1, d, tn), lambda l, j: (l, 0, j)),
                  pl.BlockSpec((1, 1, tn), lambda l, j: (l, 0, j))],
        out_specs=pl.BlockSpec((1, rows, tn), lambda l, j: (l, 0, j)),
        out_shape=jax.ShapeDtypeStruct((depth, rows, nd), F32),
        compiler_params=_cparams(2),
        name="modulation",
    )(cc, w_mod, b_mod.reshape(depth, 1, nd))


def _rms(x, w, eps):
    return x * lax.rsqrt(jnp.mean(x * x, axis=-1, keepdims=True) + eps) * w


def _ffn_chunks(d_ff):
    step = 3 * MXU_DIM
    out, c = [], 0
    while c < d_ff:
        w = min(step, d_ff - c)
        out.append((c, w))
        c += w
    return out


def _ffn_kernel(*refs, sub, d_ff, emit, n_h, n_a, lat_tiles):
    h_refs = refs[:n_h]
    mod_ref, nw_ref, win_ref, wout_ref, fin_ref = refs[n_h:n_h + 5]
    pos = n_h + 5
    a_refs = refs[pos:pos + n_a]
    pos += n_a
    if n_a:
        wp_ref = refs[pos]
        pos += 1
    out_refs = refs[pos:]

    shift = mod_ref[0, 3 * sub:3 * sub + 1, :]
    scale = mod_ref[0, 3 * sub + 1:3 * sub + 2, :]
    gate = mod_ref[0, 3 * sub + 2:3 * sub + 3, :]
    tm = out_refs[0].shape[0]
    groups = [slice(r0, r0 + tm // FFN_ROW_GROUPS) for r0 in range(0, tm, tm // FFN_ROW_GROUPS)]

    xs, us = [], []
    for rs in groups:
        x = _pick_source(h_refs, lat_tiles, rs)
        if n_a:
            x = x + mod_ref[0, 5:6, :] * _dot(_pick_source(a_refs, lat_tiles, rs), wp_ref[...])
        xs.append(x)
        us.append((_rms(x, nw_ref[sub:sub + 1, :], RMS_EPS) * (1.0 + scale) + shift).astype(BF16))
    for rs, x, u in zip(groups, xs, us):
        acc = jnp.zeros(x.shape, F32)
        for c0, cw in _ffn_chunks(d_ff):
            g = _dot(u, win_ref[:, c0:c0 + cw])
            up = _dot(u, win_ref[:, d_ff + c0:d_ff + c0 + cw])
            hid = (_silu(g) * up).astype(BF16)
            acc = acc + _dot(hid, wout_ref[c0:c0 + cw, :])
        y = x + 0.5 * gate * acc
        if emit == "mixer_in":
            out_refs[0][rs, :] = y
            u2 = _rms(y, nw_ref[1:2, :], RMS_EPS) * (1.0 + mod_ref[0, 4:5, :]) + mod_ref[0, 3:4, :]
            out_refs[1][rs, :] = u2.astype(BF16)
        elif emit == "final":
            out_refs[0][rs, :] = _rms(y, fin_ref[...], RMS_EPS)
        else:
            out_refs[0][rs, :] = y


def _stacked_weight(w, index):
    lead = len(index)
    zeros = (0,) * (w.ndim - lead)
    return pl.BlockSpec((None,) * lead + w.shape[lead:], lambda *_: tuple(index) + zeros,
                        pipeline_mode=pl.Buffered(1))


def _half_ffn(rows, n_tiles, h_srcs, mod, nw, w_in, w_out, w_index, fin_w, *, sub, emit,
              a_srcs=(), w_proj=None):
    d = h_srcs[0].shape[1]
    d_ff = w_out.shape[-2]
    tm = rows.tile
    row_spec = pl.BlockSpec((tm, d), lambda i: (i, 0))
    out_rows = n_tiles * tm
    out_shape = [jax.ShapeDtypeStruct((out_rows, d), F32)]
    out_specs = [row_spec]
    if emit == "mixer_in":
        out_shape.append(jax.ShapeDtypeStruct((out_rows, d), BF16))
        out_specs.append(row_spec)
    in_specs = (rows.source_specs(len(h_srcs), d)
                + [pl.BlockSpec((1, N_MOD, d), lambda i: (rows.mod_row(i), 0, 0)),
                   _resident(nw.shape), _stacked_weight(w_in, w_index),
                   _stacked_weight(w_out, w_index),
                   _resident(fin_w.shape)])
    args = list(h_srcs) + [mod, nw, w_in, w_out, fin_w]
    if a_srcs:
        in_specs += rows.source_specs(len(a_srcs), a_srcs[0].shape[1]) + [_resident(w_proj.shape)]
        args += list(a_srcs) + [w_proj]
    return pl.pallas_call(
        functools.partial(_ffn_kernel, sub=sub, d_ff=d_ff, emit=emit, n_h=len(h_srcs),
                          n_a=len(a_srcs), lat_tiles=rows.lat_tiles),
        grid=(n_tiles,),
        in_specs=in_specs,
        out_specs=out_specs,
        out_shape=out_shape,
        compiler_params=_cparams(1),
        name=f"half_ffn_{emit}_{len(h_srcs)}h{len(a_srcs)}a",
    )(*args)


def _rope_tables(seq, pad_rows):
    pos = jnp.arange(seq)
    row_pos = (pos // GRID_W).astype(F32)
    col_pos = (pos % GRID_W).astype(F32)
    half = ATTN_HEAD_DIM // 2
    inv_freq = 1.0 / (ROPE_THETA ** (jnp.arange(0, half, 2, dtype=F32) / half))
    ang = jnp.stack([row_pos[:, None] * inv_freq, col_pos[:, None] * inv_freq], axis=1)
    cos = jnp.cos(ang)
    sin = jnp.sin(ang)
    cos64 = jnp.stack([cos, cos], axis=2).reshape(seq, ATTN_HEAD_DIM)
    sin64 = jnp.stack([-sin, sin], axis=2).reshape(seq, ATTN_HEAD_DIM)
    cos128 = jnp.concatenate([cos64, cos64], axis=1)
    sin128 = jnp.concatenate([sin64, sin64], axis=1)
    cos128 = jnp.concatenate([cos128, jnp.ones((pad_rows, LANES), F32)], axis=0)
    sin128 = jnp.concatenate([sin128, jnp.zeros((pad_rows, LANES), F32)], axis=0)
    return cos128, sin128


def _rope(x, cos, sin, first_half):
    fwd = pltpu.roll(x, LANES - 16, axis=1)
    bwd = pltpu.roll(x, 16, axis=1)
    return x * cos + jnp.where(first_half, fwd, bwd) * sin


def _qkv_kernel(u_ref, w_ref, cos_ref, sin_ref, q_ref, k_ref, v_ref):
    u = u_ref[...]
    d = u.shape[1]
    cos = cos_ref[...]
    sin = sin_ref[...]
    lane = lax.broadcasted_iota(jnp.int32, cos.shape, 1)
    first_half = (lane % 32) < 16
    q = _dot(u, w_ref[:, 0:d])
    k = _dot(u, w_ref[:, d:2 * d])
    v = _dot(u, w_ref[:, 2 * d:3 * d]).astype(BF16)
    qk_scale = ATTN_HEAD_DIM ** -0.5 * math.log2(math.e)
    ones = jnp.ones((u.shape[0], LANES), BF16)
    for hd in range(d // LANES):
        sl = slice(hd * LANES, (hd + 1) * LANES)
        q_ref[:, sl] = (_rope(q[:, sl], cos, sin, first_half) * qk_scale).astype(BF16)
        k_ref[:, sl] = _rope(k[:, sl], cos, sin, first_half).astype(BF16)
        v_ref[:, 2 * hd * LANES:(2 * hd + 1) * LANES] = v[:, sl]
        v_ref[:, (2 * hd + 1) * LANES:(2 * hd + 2) * LANES] = ones


def _attn_qkv(rows, u, w_qkv, cos, sin):
    d = u.shape[1]
    tm = rows.tile
    row_spec = pl.BlockSpec((tm, d), lambda i: (i, 0))
    tab_spec = pl.BlockSpec(
        (tm, LANES),
        lambda i: (jnp.where(i < rows.lat_tiles, i % rows.tiles_per_seq, rows.tiles_per_seq), 0))
    return pl.pallas_call(
        _qkv_kernel,
        grid=(rows.tiles,),
        in_specs=[row_spec, _resident(w_qkv.shape), tab_spec, tab_spec],
        out_specs=[row_spec, row_spec, pl.BlockSpec((tm, 2 * d), lambda i: (i, 0))],
        out_shape=[jax.ShapeDtypeStruct(u.shape, BF16), jax.ShapeDtypeStruct(u.shape, BF16),
                   jax.ShapeDtypeStruct((u.shape[0], 2 * d), BF16)],
        compiler_params=_cparams(1),
        name="attn_qkv",
    )(u, w_qkv, cos, sin)


def _attn_kernel(lam_ref, subln_ref, q_ref, *refs, n_seg, lambda_init, rows_per_chain):
    k_refs = refs[0:n_seg]
    v_refs = refs[n_seg:2 * n_seg]
    o_ref = refs[2 * n_seg]
    lv = lam_ref[...]
    lam = (jnp.exp(jnp.sum(lv[0:1] * lv[1:2], axis=-1, keepdims=True))
           - jnp.exp(jnp.sum(lv[2:3] * lv[3:4], axis=-1, keepdims=True)) + lambda_init)
    tq = q_ref.shape[0]
    for r0 in range(0, tq, rows_per_chain):
        q = q_ref[r0:r0 + rows_per_chain, :]
        lane = lax.broadcasted_iota(jnp.int32, q.shape, 1)
        zero = jnp.zeros_like(q)
        q_maps = (jnp.where(lane < ATTN_HEAD_DIM, q, zero), jnp.where(lane >= ATTN_HEAD_DIM, q, zero))
        outs = []
        for qm in q_maps:
            s = [_dot_nt(qm, k_ref[...]) for k_ref in k_refs]
            m = functools.reduce(jnp.maximum, [jnp.max(x, axis=-1, keepdims=True) for x in s])
            acc = None
            for x, v_ref in zip(s, v_refs):
                pv = _dot(jnp.exp2(x - m).astype(BF16), v_ref[...])
                acc = pv if acc is None else acc + pv
            outs.append(acc[:, 0:ATTN_V_DIM] / acc[:, ATTN_V_DIM:2 * ATTN_V_DIM])
        o = outs[0] - lam * outs[1]
        o_ref[r0:r0 + rows_per_chain, :] = (
            _rms(o, subln_ref[...], SUBLN_EPS) * (1.0 - lambda_init)).astype(BF16)


def _attention(q, k, v, lam, subln_w, *, batch, q_len, q_row0, segs, tq, lambda_init):
    d = q.shape[1]
    n_seg = len(segs)
    q_blk0 = q_row0 // tq
    q_tiles = q_len // tq

    def q_map(b, h, t):
        return (q_blk0 + b * q_tiles + t, h)

    def o_map(b, h, t):
        return (b * q_tiles + t, h)

    k_specs, v_specs = [], []
    for first, length in segs:
        assert first % length == 0
        kv_map = lambda b, h, t, f=first // length: (f + b, h)
        k_specs.append(pl.BlockSpec((length, LANES), kv_map))
        v_specs.append(pl.BlockSpec((length, 2 * LANES), kv_map))
    return pl.pallas_call(
        functools.partial(_attn_kernel, n_seg=n_seg, lambda_init=lambda_init,
                          rows_per_chain=min(tq, ATTN_CHAIN_ROWS)),
        grid=(batch, ATTN_HEADS, q_tiles),
        in_specs=[pl.BlockSpec(lam.shape, lambda b, h, t: (0, 0)),
                  pl.BlockSpec(subln_w.shape, lambda b, h, t: (0, 0)),
                  pl.BlockSpec((tq, LANES), q_map)] + k_specs + v_specs,
        out_specs=pl.BlockSpec((tq, LANES), o_map),
        out_shape=jax.ShapeDtypeStruct((batch * q_len, d), BF16),
        compiler_params=_cparams(3),
        name=f"diff_attention_{n_seg}seg",
    )(lam, subln_w, q, *([k] * n_seg), *([v] * n_seg))


def _log_sigmoid(x):
    return jnp.minimum(x, 0.0) - jnp.log1p(jnp.exp(-jnp.abs(x)))


HALO = 16


def _feat_kernel(u_ref, up_ref, un_ref, wup_ref, cw_ref, cb_ref, wq_ref, wk_ref, wv_ref, wg_ref, bg_ref,
                 q_ref, k_ref, v_ref, xc_ref, z_ref, g_ref, gt_ref, xe_ref,
                 *, tf, lat_tiles, lat_tps, ctx_tps, k_scale):
    i = pl.program_id(0)
    j = jnp.where(i < lat_tiles, i % lat_tps, (i - lat_tiles) % ctx_tps)
    tps = jnp.where(i < lat_tiles, lat_tps, ctx_tps)
    first = j == 0
    last = j == tps - 1
    halo = HALO
    inner = xc_ref.shape[1]
    u = u_ref[...]
    u_ext = jnp.concatenate([up_ref[...], u, un_ref[...]], axis=0)
    x_ext = _dot(u_ext, wup_ref[:, 0:inner])
    z_ref[...] = _dot(u, wup_ref[:, inner:2 * inner]).astype(BF16)
    x = x_ext[halo:halo + tf, :]
    xe_ref[0:halo, :] = jnp.where(first, 0.0, x_ext[0:halo, :])
    xe_ref[halo:halo + tf, :] = x
    xe_ref[halo + tf:2 * halo + tf, :] = jnp.where(last, 0.0, x_ext[halo + tf:2 * halo + tf, :])
    conv = jnp.zeros(x.shape, F32) + cb_ref[...]
    for t in range(CONV_W):
        off = halo - CONV_W // 2 + t
        conv = conv + cw_ref[t:t + 1, :] * xe_ref[off:off + tf, :]
    xc = _silu(conv)
    xc_ref[...] = xc.astype(BF16)
    xc_b = xc.astype(BF16)
    x_b = x.astype(BF16)
    nblk = wq_ref.shape[0]
    bw = wq_ref.shape[1]
    gates = jnp.zeros((tf, LANES), F32) + bg_ref[...]
    for part, (src, w_ref, o_ref, scl) in enumerate(
            ((xc_b, wq_ref, q_ref, 1.0), (xc_b, wk_ref, k_ref, k_scale), (x_b, wv_ref, v_ref, 1.0))):
        for jb in range(nblk):
            sl = slice(jb * bw, (jb + 1) * bw)
            y = _dot(src[:, sl], w_ref[jb])
            yb = y.astype(BF16)
            o_ref[:, sl] = yb if scl == 1.0 else (y * scl).astype(BF16)
            gates = gates + _dot(yb, wg_ref[part * inner + jb * bw:part * inner + (jb + 1) * bw, :])
    lane = lax.broadcasted_iota(jnp.int32, gates.shape, 1)
    is_forget = (lane % (2 * MLSTM_HEADS)) >= MLSTM_HEADS
    gates = jnp.where(is_forget, _log_sigmoid(gates), gates)
    g1 = pltpu.roll(gates, LANES - 2 * MLSTM_HEADS, axis=1)
    g_ref[0] = gates
    g_ref[1] = g1
    gt_ref[0] = gates.T[0:2 * SUBLANES, :]
    gt_ref[1] = g1.T[0:2 * SUBLANES, :]


def _mlstm_features(rows_feat, u, w_up, conv_w, conv_b, wq, wk, wv, wg, bg, *, k_scale):
    r, d = u.shape
    inner = w_up.shape[1] // 2
    tf = rows_feat.tile
    halo = HALO
    nh = r // halo
    u_main = pl.BlockSpec((tf, d), lambda i: (i, 0))
    u_prev = pl.BlockSpec((halo, d), lambda i: (jnp.maximum(i * (tf // halo) - 1, 0), 0))
    u_next = pl.BlockSpec((halo, d), lambda i: (jnp.minimum((i + 1) * (tf // halo), nh - 1), 0))
    main = pl.BlockSpec((tf, inner), lambda i: (i, 0))
    bf_out = jax.ShapeDtypeStruct((r, inner), BF16)
    return pl.pallas_call(
        functools.partial(_feat_kernel, tf=tf, lat_tiles=rows_feat.lat_tiles,
                          lat_tps=rows_feat.tiles_per_seq, ctx_tps=rows_feat.ctx // tf,
                          k_scale=k_scale),
        grid=(rows_feat.tiles,),
        in_specs=[u_main, u_prev, u_next, _resident(w_up.shape),
                  _resident(conv_w.shape), _resident(conv_b.shape),
                  _resident(wq.shape), _resident(wk.shape), _resident(wv.shape),
                  _resident(wg.shape), _resident(bg.shape)],
        out_specs=[main, main, main, main, main,
                   pl.BlockSpec((2, tf, LANES), lambda i: (0, i, 0)),
                   pl.BlockSpec((2, 2 * SUBLANES, tf), lambda i: (0, 0, i))],
        out_shape=[bf_out, bf_out, bf_out, bf_out, bf_out,
                   jax.ShapeDtypeStruct((2, r, LANES), F32),
                   jax.ShapeDtypeStruct((2, 2 * SUBLANES, r), F32)],
        scratch_shapes=[pltpu.VMEM((tf + 2 * halo, inner), F32)],
        compiler_params=_cparams(1),
        name="mlstm_features",
    )(u, u, u, w_up, conv_w, conv_b, wq, wk, wv, wg, bg)


def _split3(x):
    a = x.astype(BF16)
    r = x - a.astype(F32)
    b = r.astype(BF16)
    c = (r - b.astype(F32)).astype(BF16)
    return a, b, c


def _scan_kernel(q_ref, k_ref, v_ref, g_ref, gt_ref, h_ref, c_scr, cb_scr, n_scr, m_scr,
                 *, chunk, dh):
    d = pl.program_id(1)
    p = pl.program_id(2)

    @pl.when(p == 0)
    def _():
        c_scr[...] = jnp.zeros(c_scr.shape, F32)
        cb_scr[...] = jnp.zeros(cb_scr.shape, BF16)
        n_scr[...] = jnp.zeros(n_scr.shape, F32)
        m_scr[...] = jnp.zeros(m_scr.shape, F32)

    g = g_ref[0]
    gt = gt_ref[0]
    ti = lax.broadcasted_iota(jnp.int32, (chunk, chunk), 0)
    si = lax.broadcasted_iota(jnp.int32, (chunk, chunk), 1)
    sgn = 1 - 2 * d
    vis = (ti - si) * sgn >= 0
    tri = jnp.where(vis, 1.0, 0.0).astype(BF16)
    tri_t = jnp.where((si - ti) * sgn >= 0, 1.0, 0.0).astype(BF16)
    bc_all = functools.reduce(lambda a, b: a + b, [_dot(tri, t) for t in _split3(g)])
    br_all = functools.reduce(lambda a, b: a + b, [_dot(t, tri_t) for t in _split3(gt)])
    neg_inf = jnp.float32(-jnp.inf)
    for hd in range(MLSTM_HEADS):
        sl = slice(hd * dh, (hd + 1) * dh)
        q = q_ref[:, sl]
        k = k_ref[:, sl]
        v = v_ref[:, sl]
        ic = g[:, hd:hd + 1]
        fc = g[:, MLSTM_HEADS + hd:MLSTM_HEADS + hd + 1]
        bc = bc_all[:, MLSTM_HEADS + hd:MLSTM_HEADS + hd + 1]
        ir = gt[hd:hd + 1, :]
        br = br_all[MLSTM_HEADS + hd:MLSTM_HEADS + hd + 1, :]
        m_prev = m_scr[hd]
        n_vec = n_scr[hd]
        log_d = jnp.where(vis, bc - br + ir, neg_inf)
        m_inter = bc + m_prev
        m_t = jnp.maximum(m_inter, jnp.max(log_d, axis=-1, keepdims=True))
        s = _dot_nt(q, k) * jnp.exp(log_d - m_t)
        w_inter = jnp.exp(m_inter - m_t)
        qf = q.astype(F32)
        num = _dot(s.astype(BF16), v) + w_inter * _dot_nt(q, cb_scr[hd])
        den = jnp.sum(s, axis=-1, keepdims=True) + w_inter * jnp.sum(qf * n_vec, axis=-1, keepdims=True)
        h_ref[0, :, sl] = (num / jnp.maximum(jnp.abs(den), jnp.exp(-m_t))).astype(h_ref.dtype)
        b_tot = jnp.sum(fc, axis=0, keepdims=True)
        log_w = b_tot - bc + ic
        m_new = jnp.maximum(b_tot + m_prev, jnp.max(log_w, axis=0, keepdims=True))
        w_s = jnp.exp(log_w - m_new)
        decay = jnp.exp(b_tot + m_prev - m_new)
        vw = (v.astype(F32) * w_s).astype(BF16)
        c_new = decay * c_scr[hd] + _dot_tn(vw, k)
        c_scr[hd] = c_new
        cb_scr[hd] = c_new.astype(BF16)
        n_scr[hd] = decay * n_vec + jnp.sum(k.astype(F32) * w_s, axis=0, keepdims=True)
        m_scr[hd] = m_new


def _mlstm_scan(q, k, v, g, gt, *, batch, seq, ctx, chunk):
    r, inner = q.shape
    dh = inner // MLSTM_HEADS
    ncc = ctx // chunk
    ncl = seq // chunk
    lat_blocks = batch * ncl

    def blk(b, d, p):
        jc = jnp.where(d == 0, p, ncc - 1 - p)
        jl = jnp.where(d == 0, p - ncc, ncl - 1 - (p - ncc))
        return jnp.where(p < ncc, lat_blocks + b * ncc + jc, b * ncl + jl)

    row_spec = pl.BlockSpec((chunk, inner), lambda b, d, p: (blk(b, d, p), 0))
    return pl.pallas_call(
        functools.partial(_scan_kernel, chunk=chunk, dh=dh),
        grid=(batch, 2, ncc + ncl),
        in_specs=[row_spec, row_spec, row_spec,
                  pl.BlockSpec((1, chunk, LANES), lambda b, d, p: (d, blk(b, d, p), 0)),
                  pl.BlockSpec((1, 2 * SUBLANES, chunk), lambda b, d, p: (d, 0, blk(b, d, p)))],
        out_specs=pl.BlockSpec((1, chunk, inner), lambda b, d, p: (d, blk(b, d, p), 0)),
        out_shape=jax.ShapeDtypeStruct((2, r, inner), BF16),
        scratch_shapes=[pltpu.VMEM((MLSTM_HEADS, dh, dh), F32),
                        pltpu.VMEM((MLSTM_HEADS, dh, dh), BF16),
                        pltpu.VMEM((MLSTM_HEADS, 1, dh), F32),
                        pltpu.VMEM((MLSTM_HEADS, 1, 1), F32)],
        compiler_params=_cparams(3),
        name="mlstm_scan",
    )(q, k, v, g, gt)


def _mlstm_out_kernel(hd_ref, xc_ref, z_ref, skip_ref, nw_ref, a_ref, *, dh):
    for hd in range(MLSTM_HEADS):
        sl = slice(hd * dh, (hd + 1) * dh)
        hh = hd_ref[0, :, sl].astype(F32) + hd_ref[1, :, sl].astype(F32)
        mu = jnp.mean(hh, axis=-1, keepdims=True)
        cen = hh - mu
        var = jnp.mean(cen * cen, axis=-1, keepdims=True)
        hn = cen * lax.rsqrt(var + HEAD_LN_EPS) * nw_ref[:, sl]
        y = (hn + skip_ref[:, sl] * xc_ref[:, sl].astype(F32)) * _silu(z_ref[:, sl].astype(F32))
        a_ref[:, sl] = y.astype(BF16)


def _mlstm_out(rows, n_tiles, hdir, xc, z, skip, norm_w):
    inner = xc.shape[1]
    tm = rows.tile
    row_spec = pl.BlockSpec((tm, inner), lambda i: (i, 0))
    return pl.pallas_call(
        functools.partial(_mlstm_out_kernel, dh=inner // MLSTM_HEADS),
        grid=(n_tiles,),
        in_specs=[pl.BlockSpec((2, tm, inner), lambda i: (0, i, 0)), row_spec, row_spec,
                  _resident(skip.shape), _resident(norm_w.shape)],
        out_specs=row_spec,
        out_shape=jax.ShapeDtypeStruct((n_tiles * tm, inner), BF16),
        compiler_params=_cparams(1),
        name="mlstm_out",
    )(hdir, xc, z, skip, norm_w)


def _blockdiag_dense(w, block):
    g = w.shape[0]
    per = block // QKV_BLOCK
    rows = w.reshape(g // per, block, QKV_BLOCK)
    tiled = jnp.tile(rows, (1, 1, per))
    r = lax.broadcasted_iota(jnp.int32, (block, block), 0) // QKV_BLOCK
    c = lax.broadcasted_iota(jnp.int32, (block, block), 1) // QKV_BLOCK
    return jnp.where(r == c, tiled, jnp.zeros_like(tiled))


def _pick_tile(seq, ctx_rows, target):
    t = target
    while seq % t or ctx_rows % t:
        t //= 2
    return t


def kernel(x, c, ctx, c_ctx, w_mod, b_mod, norm_w, ffn_w_in, ffn_w_out,
           attn_w_qkv, attn_w_o, attn_lambda, attn_subln_w,
           mlstm_w_up, mlstm_conv_w, mlstm_conv_b, mlstm_w_qkv, mlstm_w_gates, mlstm_b_gates,
           mlstm_skip, mlstm_norm_w, mlstm_w_down, final_norm_w):
    batch, seq, d = x.shape
    ctx_len = ctx.shape[1]
    depth = w_mod.shape[0]
    inner = mlstm_w_down.shape[1]
    dh = inner // MLSTM_HEADS

    rows = _Rows(batch, seq, ctx_len, _pick_tile(seq, batch * ctx_len, 512))
    rows_feat = _Rows(batch, seq, ctx_len, _pick_tile(seq, ctx_len, 256))
    chunk = _pick_tile(seq, ctx_len, 256)
    tq = _pick_tile(seq, seq, 2048)

    n_mod_rows = -(-(batch + 1) // MOD_ROWS_PAD) * MOD_ROWS_PAD
    cc = jnp.concatenate([c, c_ctx[None, :], jnp.zeros((n_mod_rows - batch - 1, d), F32)], axis=0)
    mod_all = _modulation(cc, w_mod, b_mod).reshape(depth, n_mod_rows, N_MOD, d)

    cos, sin = _rope_tables(seq, rows.tile)
    fin_w = final_norm_w.reshape(1, d)
    h_srcs = (x.reshape(batch * seq, d), ctx.reshape(batch * ctx_len, d))
    w_in_bf = ffn_w_in.astype(BF16)
    w_out_bf = ffn_w_out.astype(BF16)

    for i in range(depth):
        last = i == depth - 1
        j = i // N_MIXERS
        mod = mod_all[i]
        nw = norm_w[i]
        h, u = _half_ffn(rows, rows.tiles, h_srcs, mod, nw, w_in_bf, w_out_bf, (i, 0),
                         fin_w, sub=0, emit="mixer_in")
        n_post = rows.lat_tiles if last else rows.tiles
        if i % N_MIXERS == 0:
            lambda_init = 0.8 - 0.6 * math.exp(-0.3 * i)
            q, k, v = _attn_qkv(rows, u, attn_w_qkv[j].astype(BF16), cos, sin)
            lam = attn_lambda[j]
            subln = attn_subln_w[j].reshape(1, ATTN_V_DIM)
            o_lat = _attention(q, k, v, lam, subln, batch=batch, q_len=seq, q_row0=0,
                               segs=[(rows.n_lat, ctx_len), (0, seq)], tq=tq,
                               lambda_init=lambda_init)
            o_ctx = _attention(q, k, v, lam, subln, batch=batch, q_len=ctx_len, q_row0=rows.n_lat,
                               segs=[(rows.n_lat, ctx_len)], tq=ctx_len, lambda_init=lambda_init)
            a_srcs = (o_lat, o_ctx)
            w_proj = attn_w_o[j].astype(BF16)
        else:
            wq = _blockdiag_dense(mlstm_w_qkv[j, 0], MXU_DIM).astype(BF16)
            wk = _blockdiag_dense(mlstm_w_qkv[j, 1], MXU_DIM).astype(BF16)
            wv = _blockdiag_dense(mlstm_w_qkv[j, 2], MXU_DIM).astype(BF16)
            n_gate = mlstm_w_gates.shape[2]
            wg = jnp.pad(mlstm_w_gates[j], ((0, 0), (0, LANES - n_gate))).astype(BF16)
            bg = jnp.pad(mlstm_b_gates[j], (0, LANES - n_gate)).reshape(1, LANES)
            q, k, v, xc, z, g, gt = _mlstm_features(
                rows_feat, u, mlstm_w_up[j].astype(BF16),
                mlstm_conv_w[j], mlstm_conv_b[j].reshape(1, inner),
                wq, wk, wv, wg, bg, k_scale=dh ** -0.5)
            hdir = _mlstm_scan(q, k, v, g, gt, batch=batch, seq=seq, ctx=ctx_len, chunk=chunk)
            a = _mlstm_out(rows, n_post, hdir, xc, z, mlstm_skip[j].reshape(1, inner),
                           mlstm_norm_w[j].reshape(1, inner))
            a_srcs = (a,)
            w_proj = mlstm_w_down[j].astype(BF16)
        (h,) = _half_ffn(rows, n_post, (h,), mod, nw, w_in_bf, w_out_bf, (i, 1),
                         fin_w, sub=2, emit="final" if last else "plain",
                         a_srcs=a_srcs, w_proj=w_proj)
        h_srcs = (h,)
    return h.reshape(batch, seq, d)
```

```python
import functools
import math

import jax
import jax.numpy as jnp
from jax import lax
from jax.experimental import pallas as pl
from jax.experimental.pallas import tpu as pltpu

N_MOD = 9
RMS_EPS = 1e-6
ATTN_HEADS = 8
ATTN_HEAD_DIM = 64
ATTN_V_DIM = 128
ROPE_THETA = 10000.0
GRID_W = 64
SUBLN_EPS = 1e-5
MLSTM_HEADS = 4
QKV_BLOCK = 4
CONV_W = 5
HEAD_LN_EPS = 1e-5
N_MIXERS = 2

LANES = 128
SUBLANES = 8
MXU_DIM = 256
MOD_ROWS_PAD = 8
ATTN_CHAIN_ROWS = 128
VMEM_LIMIT = 56 * 1024 * 1024

BF16 = jnp.bfloat16
F32 = jnp.float32


def _cparams(n_axes):
    return pltpu.CompilerParams(
        dimension_semantics=("arbitrary",) * n_axes,
        vmem_limit_bytes=VMEM_LIMIT)


def _resident(shape):
    nd = len(shape)
    return pl.BlockSpec(shape, lambda *_: (0,) * nd, pipeline_mode=pl.Buffered(1))


def _dot(a, b):
    return jnp.dot(a, b, preferred_element_type=F32)


def _dot_nt(a, b):
    return lax.dot_general(a, b, (((1,), (1,)), ((), ())), preferred_element_type=F32)


def _dot_tn(a, b):
    return lax.dot_general(a, b, (((0,), (0,)), ((), ())), preferred_element_type=F32)


def _sigmoid(x):
    return 1.0 / (1.0 + jnp.exp(-x))


def _silu(x):
    return x * _sigmoid(x)


class _Rows:
    def __init__(self, batch, seq, ctx, tile):
        assert seq % tile == 0 and (batch * ctx) % tile == 0
        self.batch, self.seq, self.ctx, self.tile = batch, seq, ctx, tile
        self.n_lat = batch * seq
        self.rows = batch * (seq + ctx)
        self.lat_tiles = self.n_lat // tile
        self.tiles = self.rows // tile
        self.tiles_per_seq = seq // tile

    def mod_row(self, i):
        return jnp.where(i < self.lat_tiles, i // self.tiles_per_seq, self.batch)

    def source_specs(self, n_src, width):
        tm = self.tile
        if n_src == 1:
            return [pl.BlockSpec((tm, width), lambda i: (i, 0))]
        lat_last = self.lat_tiles - 1
        return [pl.BlockSpec((tm, width), lambda i: (jnp.minimum(i, lat_last), 0)),
                pl.BlockSpec((tm, width), lambda i: (jnp.maximum(i - self.lat_tiles, 0), 0))]


def _pick_source(refs, lat_tiles):
    if len(refs) == 1:
        return refs[0][...]
    return jnp.where(pl.program_id(0) < lat_tiles, refs[0][...], refs[1][...])


def _mod_kernel(c_ref, w_ref, b_ref, o_ref):
    s = _silu(c_ref[...])
    o_ref[0] = jnp.dot(s, w_ref[0], preferred_element_type=F32,
                       precision=lax.Precision.HIGHEST) + b_ref[0]


def _modulation(cc, w_mod, b_mod):
    depth, d, nd = w_mod.shape
    rows = cc.shape[0]
    tn = d
    return pl.pallas_call(
        _mod_kernel,
        grid=(depth, nd // tn),
        in_specs=[pl.BlockSpec((rows, d), lambda l, j: (0, 0)),
                  pl.BlockSpec((1, d, tn), lambda l, j: (l, 0, j)),
                  pl.BlockSpec((1, 1, tn), lambda l, j: (l, 0, j))],
        out_specs=pl.BlockSpec((1, rows, tn), lambda l, j: (l, 0, j)),
        out_shape=jax.ShapeDtypeStruct((depth, rows, nd), F32),
        compiler_params=_cparams(2),
        name="modulation",
    )(cc, w_mod, b_mod.reshape(depth, 1, nd))


def _rms(x, w, eps):
    return x * lax.rsqrt(jnp.mean(x * x, axis=-1, keepdims=True) + eps) * w


def _ffn_chunks(d_ff):
    step = 3 * MXU_DIM
    out, c = [], 0
    while c < d_ff:
        w = min(step, d_ff - c)
        out.append((c, w))
        c += w
    return out


def _ffn_kernel(*refs, sub, d_ff, emit, n_h, n_a, lat_tiles):
    h_refs = refs[:n_h]
    mod_ref, nw_ref, win_ref, wout_ref, fin_ref = refs[n_h:n_h + 5]
    pos = n_h + 5
    a_refs = refs[pos:pos + n_a]
    pos += n_a
    if n_a:
        wp_ref = refs[pos]
        pos += 1
    out_refs = refs[pos:]

    x = _pick_source(h_refs, lat_tiles)
    if n_a:
        x = x + mod_ref[0, 5:6, :] * _dot(_pick_source(a_refs, lat_tiles), wp_ref[...])
    shift = mod_ref[0, 3 * sub:3 * sub + 1, :]
    scale = mod_ref[0, 3 * sub + 1:3 * sub + 2, :]
    gate = mod_ref[0, 3 * sub + 2:3 * sub + 3, :]
    u = (_rms(x, nw_ref[sub:sub + 1, :], RMS_EPS) * (1.0 + scale) + shift).astype(BF16)
    acc = jnp.zeros(x.shape, F32)
    for c0, cw in _ffn_chunks(d_ff):
        g = _dot(u, win_ref[:, c0:c0 + cw])
        up = _dot(u, win_ref[:, d_ff + c0:d_ff + c0 + cw])
        hid = (_silu(g) * up).astype(BF16)
        acc = acc + _dot(hid, wout_ref[c0:c0 + cw, :])
    y = x + 0.5 * gate * acc
    if emit == "mixer_in":
        out_refs[0][...] = y
        u2 = _rms(y, nw_ref[1:2, :], RMS_EPS) * (1.0 + mod_ref[0, 4:5, :]) + mod_ref[0, 3:4, :]
        out_refs[1][...] = u2.astype(BF16)
    elif emit == "final":
        out_refs[0][...] = _rms(y, fin_ref[...], RMS_EPS)
    else:
        out_refs[0][...] = y


def _stacked_weight(w, index):
    lead = len(index)
    zeros = (0,) * (w.ndim - lead)
    return pl.BlockSpec((None,) * lead + w.shape[lead:], lambda *_: tuple(index) + zeros,
                        pipeline_mode=pl.Buffered(1))


def _half_ffn(rows, n_tiles, h_srcs, mod, nw, w_in, w_out, w_index, fin_w, *, sub, emit,
              a_srcs=(), w_proj=None):
    d = h_srcs[0].shape[1]
    d_ff = w_out.shape[-2]
    tm = rows.tile
    row_spec = pl.BlockSpec((tm, d), lambda i: (i, 0))
    out_rows = n_tiles * tm
    out_shape = [jax.ShapeDtypeStruct((out_rows, d), F32)]
    out_specs = [row_spec]
    if emit == "mixer_in":
        out_shape.append(jax.ShapeDtypeStruct((out_rows, d), BF16))
        out_specs.append(row_spec)
    in_specs = (rows.source_specs(len(h_srcs), d)
                + [pl.BlockSpec((1, N_MOD, d), lambda i: (rows.mod_row(i), 0, 0)),
                   _resident(nw.shape), _stacked_weight(w_in, w_index),
                   _stacked_weight(w_out, w_index),
                   _resident(fin_w.shape)])
    args = list(h_srcs) + [mod, nw, w_in, w_out, fin_w]
    if a_srcs:
        in_specs += rows.source_specs(len(a_srcs), a_srcs[0].shape[1]) + [_resident(w_proj.shape)]
        args += list(a_srcs) + [w_proj]
    return pl.pallas_call(
        functools.partial(_ffn_kernel, sub=sub, d_ff=d_ff, emit=emit, n_h=len(h_srcs),
                          n_a=len(a_srcs), lat_tiles=rows.lat_tiles),
        grid=(n_tiles,),
        in_specs=in_specs,
        out_specs=out_specs,
        out_shape=out_shape,
        compiler_params=_cparams(1),
        name=f"half_ffn_{emit}_{len(h_srcs)}h{len(a_srcs)}a",
    )(*args)


def _rope_tables(seq, pad_rows):
    pos = jnp.arange(seq)
    row_pos = (pos // GRID_W).astype(F32)
    col_pos = (pos % GRID_W).astype(F32)
    half = ATTN_HEAD_DIM // 2
    inv_freq = 1.0 / (ROPE_THETA ** (jnp.arange(0, half, 2, dtype=F32) / half))
    ang = jnp.stack([row_pos[:, None] * inv_freq, col_pos[:, None] * inv_freq], axis=1)
    cos = jnp.cos(ang)
    sin = jnp.sin(ang)
    cos64 = jnp.stack([cos, cos], axis=2).reshape(seq, ATTN_HEAD_DIM)
    sin64 = jnp.stack([-sin, sin], axis=2).reshape(seq, ATTN_HEAD_DIM)
    cos128 = jnp.concatenate([cos64, cos64], axis=1)
    sin128 = jnp.concatenate([sin64, sin64], axis=1)
    cos128 = jnp.concatenate([cos128, jnp.ones((pad_rows, LANES), F32)], axis=0)
    sin128 = jnp.concatenate([sin128, jnp.zeros((pad_rows, LANES), F32)], axis=0)
    return cos128, sin128


def _rope(x, cos, sin, first_half):
    fwd = pltpu.roll(x, LANES - 16, axis=1)
    bwd = pltpu.roll(x, 16, axis=1)
    return x * cos + jnp.where(first_half, fwd, bwd) * sin


def _qkv_kernel(u_ref, w_ref, cos_ref, sin_ref, q_ref, k_ref, v_ref):
    u = u_ref[...]
    d = u.shape[1]
    cos = cos_ref[...]
    sin = sin_ref[...]
    lane = lax.broadcasted_iota(jnp.int32, cos.shape, 1)
    first_half = (lane % 32) < 16
    q = _dot(u, w_ref[:, 0:d])
    k = _dot(u, w_ref[:, d:2 * d])
    v = _dot(u, w_ref[:, 2 * d:3 * d]).astype(BF16)
    qk_scale = ATTN_HEAD_DIM ** -0.5 * math.log2(math.e)
    ones = jnp.ones((u.shape[0], LANES), BF16)
    for hd in range(d // LANES):
        sl = slice(hd * LANES, (hd + 1) * LANES)
        q_ref[:, sl] = (_rope(q[:, sl], cos, sin, first_half) * qk_scale).astype(BF16)
        k_ref[:, sl] = _rope(k[:, sl], cos, sin, first_half).astype(BF16)
        v_ref[:, 2 * hd * LANES:(2 * hd + 1) * LANES] = v[:, sl]
        v_ref[:, (2 * hd + 1) * LANES:(2 * hd + 2) * LANES] = ones


def _attn_qkv(rows, u, w_qkv, cos, sin):
    d = u.shape[1]
    tm = rows.tile
    row_spec = pl.BlockSpec((tm, d), lambda i: (i, 0))
    tab_spec = pl.BlockSpec(
        (tm, LANES),
        lambda i: (jnp.where(i < rows.lat_tiles, i % rows.tiles_per_seq, rows.tiles_per_seq), 0))
    return pl.pallas_call(
        _qkv_kernel,
        grid=(rows.tiles,),
        in_specs=[row_spec, _resident(w_qkv.shape), tab_spec, tab_spec],
        out_specs=[row_spec, row_spec, pl.BlockSpec((tm, 2 * d), lambda i: (i, 0))],
        out_shape=[jax.ShapeDtypeStruct(u.shape, BF16), jax.ShapeDtypeStruct(u.shape, BF16),
                   jax.ShapeDtypeStruct((u.shape[0], 2 * d), BF16)],
        compiler_params=_cparams(1),
        name="attn_qkv",
    )(u, w_qkv, cos, sin)


def _attn_kernel(lam_ref, subln_ref, q_ref, *refs, n_seg, lambda_init, rows_per_chain):
    k_refs = refs[0:n_seg]
    v_refs = refs[n_seg:2 * n_seg]
    o_ref = refs[2 * n_seg]
    lv = lam_ref[...]
    lam = (jnp.exp(jnp.sum(lv[0:1] * lv[1:2], axis=-1, keepdims=True))
           - jnp.exp(jnp.sum(lv[2:3] * lv[3:4], axis=-1, keepdims=True)) + lambda_init)
    tq = q_ref.shape[0]
    for r0 in range(0, tq, rows_per_chain):
        q = q_ref[r0:r0 + rows_per_chain, :]
        lane = lax.broadcasted_iota(jnp.int32, q.shape, 1)
        zero = jnp.zeros_like(q)
        q_maps = (jnp.where(lane < ATTN_HEAD_DIM, q, zero), jnp.where(lane >= ATTN_HEAD_DIM, q, zero))
        outs = []
        for qm in q_maps:
            s = [_dot_nt(qm, k_ref[...]) for k_ref in k_refs]
            m = functools.reduce(jnp.maximum, [jnp.max(x, axis=-1, keepdims=True) for x in s])
            acc = None
            for x, v_ref in zip(s, v_refs):
                pv = _dot(jnp.exp2(x - m).astype(BF16), v_ref[...])
                acc = pv if acc is None else acc + pv
            outs.append(acc[:, 0:ATTN_V_DIM] / acc[:, ATTN_V_DIM:2 * ATTN_V_DIM])
        o = outs[0] - lam * outs[1]
        o_ref[r0:r0 + rows_per_chain, :] = (
            _rms(o, subln_ref[...], SUBLN_EPS) * (1.0 - lambda_init)).astype(BF16)


def _attention(q, k, v, lam, subln_w, *, batch, q_len, q_row0, segs, tq, lambda_init):
    d = q.shape[1]
    n_seg = len(segs)
    q_blk0 = q_row0 // tq
    q_tiles = q_len // tq

    def q_map(b, h, t):
        return (q_blk0 + b * q_tiles + t, h)

    def o_map(b, h, t):
        return (b * q_tiles + t, h)

    k_specs, v_specs = [], []
    for first, length in segs:
        assert first % length == 0
        kv_map = lambda b, h, t, f=first // length: (f + b, h)
        k_specs.append(pl.BlockSpec((length, LANES), kv_map))
        v_specs.append(pl.BlockSpec((length, 2 * LANES), kv_map))
    return pl.pallas_call(
        functools.partial(_attn_kernel, n_seg=n_seg, lambda_init=lambda_init,
                          rows_per_chain=min(tq, ATTN_CHAIN_ROWS)),
        grid=(batch, ATTN_HEADS, q_tiles),
        in_specs=[pl.BlockSpec(lam.shape, lambda b, h, t: (0, 0)),
                  pl.BlockSpec(subln_w.shape, lambda b, h, t: (0, 0)),
                  pl.BlockSpec((tq, LANES), q_map)] + k_specs + v_specs,
        out_specs=pl.BlockSpec((tq, LANES), o_map),
        out_shape=jax.ShapeDtypeStruct((batch * q_len, d), BF16),
        compiler_params=_cparams(3),
        name=f"diff_attention_{n_seg}seg",
    )(lam, subln_w, q, *([k] * n_seg), *([v] * n_seg))


def _log_sigmoid(x):
    return jnp.minimum(x, 0.0) - jnp.log1p(jnp.exp(-jnp.abs(x)))


HALO = 16


def _feat_kernel(u_ref, up_ref, un_ref, wup_ref, cw_ref, cb_ref, wq_ref, wk_ref, wv_ref, wg_ref, bg_ref,
                 q_ref, k_ref, v_ref, xc_ref, z_ref, g_ref, gt_ref, xe_ref,
                 *, tf, lat_tiles, lat_tps, ctx_tps, k_scale):
    i = pl.program_id(0)
    j = jnp.where(i < lat_tiles, i % lat_tps, (i - lat_tiles) % ctx_tps)
    tps = jnp.where(i < lat_tiles, lat_tps, ctx_tps)
    first = j == 0
    last = j == tps - 1
    halo = HALO
    inner = xc_ref.shape[1]
    u = u_ref[...]
    u_ext = jnp.concatenate([up_ref[...], u, un_ref[...]], axis=0)
    x_ext = _dot(u_ext, wup_ref[:, 0:inner])
    z_ref[...] = _dot(u, wup_ref[:, inner:2 * inner]).astype(BF16)
    x = x_ext[halo:halo + tf, :]
    xe_ref[0:halo, :] = jnp.where(first, 0.0, x_ext[0:halo, :])
    xe_ref[halo:halo + tf, :] = x
    xe_ref[halo + tf:2 * halo + tf, :] = jnp.where(last, 0.0, x_ext[halo + tf:2 * halo + tf, :])
    conv = jnp.zeros(x.shape, F32) + cb_ref[...]
    for t in range(CONV_W):
        off = halo - CONV_W // 2 + t
        conv = conv + cw_ref[t:t + 1, :] * xe_ref[off:off + tf, :]
    xc = _silu(conv)
    xc_ref[...] = xc.astype(BF16)
    xc_b = xc.astype(BF16)
    x_b = x.astype(BF16)
    nblk = wq_ref.shape[0]
    bw = wq_ref.shape[1]
    gates = jnp.zeros((tf, LANES), F32) + bg_ref[...]
    for part, (src, w_ref, o_ref, scl) in enumerate(
            ((xc_b, wq_ref, q_ref, 1.0), (xc_b, wk_ref, k_ref, k_scale), (x_b, wv_ref, v_ref, 1.0))):
        for jb in range(nblk):
            sl = slice(jb * bw, (jb + 1) * bw)
            y = _dot(src[:, sl], w_ref[jb])
            yb = y.astype(BF16)
            o_ref[:, sl] = yb if scl == 1.0 else (y * scl).astype(BF16)
            gates = gates + _dot(yb, wg_ref[part * inner + jb * bw:part * inner + (jb + 1) * bw, :])
    lane = lax.broadcasted_iota(jnp.int32, gates.shape, 1)
    is_forget = (lane % (2 * MLSTM_HEADS)) >= MLSTM_HEADS
    gates = jnp.where(is_forget, _log_sigmoid(gates), gates)
    g1 = pltpu.roll(gates, LANES - 2 * MLSTM_HEADS, axis=1)
    g_ref[0] = gates
    g_ref[1] = g1
    gt_ref[0] = gates.T[0:2 * SUBLANES, :]
    gt_ref[1] = g1.T[0:2 * SUBLANES, :]


def _mlstm_features(rows_feat, u, w_up, conv_w, conv_b, wq, wk, wv, wg, bg, *, k_scale):
    r, d = u.shape
    inner = w_up.shape[1] // 2
    tf = rows_feat.tile
    halo = HALO
    nh = r // halo
    u_main = pl.BlockSpec((tf, d), lambda i: (i, 0))
    u_prev = pl.BlockSpec((halo, d), lambda i: (jnp.maximum(i * (tf // halo) - 1, 0), 0))
    u_next = pl.BlockSpec((halo, d), lambda i: (jnp.minimum((i + 1) * (tf // halo), nh - 1), 0))
    main = pl.BlockSpec((tf, inner), lambda i: (i, 0))
    bf_out = jax.ShapeDtypeStruct((r, inner), BF16)
    return pl.pallas_call(
        functools.partial(_feat_kernel, tf=tf, lat_tiles=rows_feat.lat_tiles,
                          lat_tps=rows_feat.tiles_per_seq, ctx_tps=rows_feat.ctx // tf,
                          k_scale=k_scale),
        grid=(rows_feat.tiles,),
        in_specs=[u_main, u_prev, u_next, _resident(w_up.shape),
                  _resident(conv_w.shape), _resident(conv_b.shape),
                  _resident(wq.shape), _resident(wk.shape), _resident(wv.shape),
                  _resident(wg.shape), _resident(bg.shape)],
        out_specs=[main, main, main, main, main,
                   pl.BlockSpec((2, tf, LANES), lambda i: (0, i, 0)),
                   pl.BlockSpec((2, 2 * SUBLANES, tf), lambda i: (0, 0, i))],
        out_shape=[bf_out, bf_out, bf_out, bf_out, bf_out,
                   jax.ShapeDtypeStruct((2, r, LANES), F32),
                   jax.ShapeDtypeStruct((2, 2 * SUBLANES, r), F32)],
        scratch_shapes=[pltpu.VMEM((tf + 2 * halo, inner), F32)],
        compiler_params=_cparams(1),
        name="mlstm_features",
    )(u, u, u, w_up, conv_w, conv_b, wq, wk, wv, wg, bg)


def _split3(x):
    a = x.astype(BF16)
    r = x - a.astype(F32)
    b = r.astype(BF16)
    c = (r - b.astype(F32)).astype(BF16)
    return a, b, c


def _scan_kernel(q_ref, k_ref, v_ref, g_ref, gt_ref, h_ref, c_scr, cb_scr, n_scr, m_scr,
                 *, chunk, dh):
    d = pl.program_id(1)
    p = pl.program_id(2)

    @pl.when(p == 0)
    def _():
        c_scr[...] = jnp.zeros(c_scr.shape, F32)
        cb_scr[...] = jnp.zeros(cb_scr.shape, BF16)
        n_scr[...] = jnp.zeros(n_scr.shape, F32)
        m_scr[...] = jnp.zeros(m_scr.shape, F32)

    g = g_ref[0]
    gt = gt_ref[0]
    ti = lax.broadcasted_iota(jnp.int32, (chunk, chunk), 0)
    si = lax.broadcasted_iota(jnp.int32, (chunk, chunk), 1)
    sgn = 1 - 2 * d
    vis = (ti - si) * sgn >= 0
    tri = jnp.where(vis, 1.0, 0.0).astype(BF16)
    tri_t = jnp.where((si - ti) * sgn >= 0, 1.0, 0.0).astype(BF16)
    bc_all = functools.reduce(lambda a, b: a + b, [_dot(tri, t) for t in _split3(g)])
    br_all = functools.reduce(lambda a, b: a + b, [_dot(t, tri_t) for t in _split3(gt)])
    neg_inf = jnp.float32(-jnp.inf)
    for hd in range(MLSTM_HEADS):
        sl = slice(hd * dh, (hd + 1) * dh)
        q = q_ref[:, sl]
        k = k_ref[:, sl]
        v = v_ref[:, sl]
        ic = g[:, hd:hd + 1]
        fc = g[:, MLSTM_HEADS + hd:MLSTM_HEADS + hd + 1]
        bc = bc_all[:, MLSTM_HEADS + hd:MLSTM_HEADS + hd + 1]
        ir = gt[hd:hd + 1, :]
        br = br_all[MLSTM_HEADS + hd:MLSTM_HEADS + hd + 1, :]
        m_prev = m_scr[hd]
        n_vec = n_scr[hd]
        log_d = jnp.where(vis, bc - br + ir, neg_inf)
        m_inter = bc + m_prev
        m_t = jnp.maximum(m_inter, jnp.max(log_d, axis=-1, keepdims=True))
        s = _dot_nt(q, k) * jnp.exp(log_d - m_t)
        w_inter = jnp.exp(m_inter - m_t)
        qf = q.astype(F32)
        num = _dot(s.astype(BF16), v) + w_inter * _dot_nt(q, cb_scr[hd])
        den = jnp.sum(s, axis=-1, keepdims=True) + w_inter * jnp.sum(qf * n_vec, axis=-1, keepdims=True)
        h_ref[0, :, sl] = (num / jnp.maximum(jnp.abs(den), jnp.exp(-m_t))).astype(h_ref.dtype)
        b_tot = jnp.sum(fc, axis=0, keepdims=True)
        log_w = b_tot - bc + ic
        m_new = jnp.maximum(b_tot + m_prev, jnp.max(log_w, axis=0, keepdims=True))
        w_s = jnp.exp(log_w - m_new)
        decay = jnp.exp(b_tot + m_prev - m_new)
        vw = (v.astype(F32) * w_s).astype(BF16)
        c_new = decay * c_scr[hd] + _dot_tn(vw, k)
        c_scr[hd] = c_new
        cb_scr[hd] = c_new.astype(BF16)
        n_scr[hd] = decay * n_vec + jnp.sum(k.astype(F32) * w_s, axis=0, keepdims=True)
        m_scr[hd] = m_new


def _mlstm_scan(q, k, v, g, gt, *, batch, seq, ctx, chunk):
    r, inner = q.shape
    dh = inner // MLSTM_HEADS
    ncc = ctx // chunk
    ncl = seq // chunk
    lat_blocks = batch * ncl

    def blk(b, d, p):
        jc = jnp.where(d == 0, p, ncc - 1 - p)
        jl = jnp.where(d == 0, p - ncc, ncl - 1 - (p - ncc))
        return jnp.where(p < ncc, lat_blocks + b * ncc + jc, b * ncl + jl)

    row_spec = pl.BlockSpec((chunk, inner), lambda b, d, p: (blk(b, d, p), 0))
    return pl.pallas_call(
        functools.partial(_scan_kernel, chunk=chunk, dh=dh),
        grid=(batch, 2, ncc + ncl),
        in_specs=[row_spec, row_spec, row_spec,
                  pl.BlockSpec((1, chunk, LANES), lambda b, d, p: (d, blk(b, d, p), 0)),
                  pl.BlockSpec((1, 2 * SUBLANES, chunk), lambda b, d, p: (d, 0, blk(b, d, p)))],
        out_specs=pl.BlockSpec((1, chunk, inner), lambda b, d, p: (d, blk(b, d, p), 0)),
        out_shape=jax.ShapeDtypeStruct((2, r, inner), BF16),
        scratch_shapes=[pltpu.VMEM((MLSTM_HEADS, dh, dh), F32),
                        pltpu.VMEM((MLSTM_HEADS, dh, dh), BF16),
                        pltpu.VMEM((MLSTM_HEADS, 1, dh), F32),
                        pltpu.VMEM((MLSTM_HEADS, 1, 1), F32)],
        compiler_params=_cparams(3),
        name="mlstm_scan",
    )(q, k, v, g, gt)


def _mlstm_out_kernel(hd_ref, xc_ref, z_ref, skip_ref, nw_ref, a_ref, *, dh):
    for hd in range(MLSTM_HEADS):
        sl = slice(hd * dh, (hd + 1) * dh)
        hh = hd_ref[0, :, sl].astype(F32) + hd_ref[1, :, sl].astype(F32)
        mu = jnp.mean(hh, axis=-1, keepdims=True)
        cen = hh - mu
        var = jnp.mean(cen * cen, axis=-1, keepdims=True)
        hn = cen * lax.rsqrt(var + HEAD_LN_EPS) * nw_ref[:, sl]
        y = (hn + skip_ref[:, sl] * xc_ref[:, sl].astype(F32)) * _silu(z_ref[:, sl].astype(F32))
        a_ref[:, sl] = y.astype(BF16)


def _mlstm_out(rows, n_tiles, hdir, xc, z, skip, norm_w):
    inner = xc.shape[1]
    tm = rows.tile
    row_spec = pl.BlockSpec((tm, inner), lambda i: (i, 0))
    return pl.pallas_call(
        functools.partial(_mlstm_out_kernel, dh=inner // MLSTM_HEADS),
        grid=(n_tiles,),
        in_specs=[pl.BlockSpec((2, tm, inner), lambda i: (0, i, 0)), row_spec, row_spec,
                  _resident(skip.shape), _resident(norm_w.shape)],
        out_specs=row_spec,
        out_shape=jax.ShapeDtypeStruct((n_tiles * tm, inner), BF16),
        compiler_params=_cparams(1),
        name="mlstm_out",
    )(hdir, xc, z, skip, norm_w)


def _blockdiag_dense(w, block):
    g = w.shape[0]
    per = block // QKV_BLOCK
    rows = w.reshape(g // per, block, QKV_BLOCK)
    tiled = jnp.tile(rows, (1, 1, per))
    r = lax.broadcasted_iota(jnp.int32, (block, block), 0) // QKV_BLOCK
    c = lax.broadcasted_iota(jnp.int32, (block, block), 1) // QKV_BLOCK
    return jnp.where(r == c, tiled, jnp.zeros_like(tiled))


def _pick_tile(seq, ctx_rows, target):
    t = target
    while seq % t or ctx_rows % t:
        t //= 2
    return t


def kernel(x, c, ctx, c_ctx, w_mod, b_mod, norm_w, ffn_w_in, ffn_w_out,
           attn_w_qkv, attn_w_o, attn_lambda, attn_subln_w,
           mlstm_w_up, mlstm_conv_w, mlstm_conv_b, mlstm_w_qkv, mlstm_w_gates, mlstm_b_gates,
           mlstm_skip, mlstm_norm_w, mlstm_w_down, final_norm_w):
    batch, seq, d = x.shape
    ctx_len = ctx.shape[1]
    depth = w_mod.shape[0]
    inner = mlstm_w_down.shape[1]
    dh = inner // MLSTM_HEADS

    rows = _Rows(batch, seq, ctx_len, _pick_tile(seq, batch * ctx_len, 512))
    rows_feat = _Rows(batch, seq, ctx_len, _pick_tile(seq, ctx_len, 256))
    chunk = _pick_tile(seq, ctx_len, 256)
    tq = _pick_tile(seq, seq, 2048)

    n_mod_rows = -(-(batch + 1) // MOD_ROWS_PAD) * MOD_ROWS_PAD
    cc = jnp.concatenate([c, c_ctx[None, :], jnp.zeros((n_mod_rows - batch - 1, d), F32)], axis=0)
    mod_all = _modulation(cc, w_mod, b_mod).reshape(depth, n_mod_rows, N_MOD, d)

    cos, sin = _rope_tables(seq, rows.tile)
    fin_w = final_norm_w.reshape(1, d)
    h_srcs = (x.reshape(batch * seq, d), ctx.reshape(batch * ctx_len, d))
    w_in_bf = ffn_w_in.astype(BF16)
    w_out_bf = ffn_w_out.astype(BF16)

    for i in range(depth):
        last = i == depth - 1
        j = i // N_MIXERS
        mod = mod_all[i]
        nw = norm_w[i]
        h, u = _half_ffn(rows, rows.tiles, h_srcs, mod, nw, w_in_bf, w_out_bf, (i, 0),
                         fin_w, sub=0, emit="mixer_in")
        n_post = rows.lat_tiles if last else rows.tiles
        if i % N_MIXERS == 0:
            lambda_init = 0.8 - 0.6 * math.exp(-0.3 * i)
            q, k, v = _attn_qkv(rows, u, attn_w_qkv[j].astype(BF16), cos, sin)
            lam = attn_lambda[j]
            subln = attn_subln_w[j].reshape(1, ATTN_V_DIM)
            o_lat = _attention(q, k, v, lam, subln, batch=batch, q_len=seq, q_row0=0,
                               segs=[(rows.n_lat, ctx_len), (0, seq)], tq=tq,
                               lambda_init=lambda_init)
            o_ctx = _attention(q, k, v, lam, subln, batch=batch, q_len=ctx_len, q_row0=rows.n_lat,
                               segs=[(rows.n_lat, ctx_len)], tq=ctx_len, lambda_init=lambda_init)
            a_srcs = (o_lat, o_ctx)
            w_proj = attn_w_o[j].astype(BF16)
        else:
            wq = _blockdiag_dense(mlstm_w_qkv[j, 0], MXU_DIM).astype(BF16)
            wk = _blockdiag_dense(mlstm_w_qkv[j, 1], MXU_DIM).astype(BF16)
            wv = _blockdiag_dense(mlstm_w_qkv[j, 2], MXU_DIM).astype(BF16)
            n_gate = mlstm_w_gates.shape[2]
            wg = jnp.pad(mlstm_w_gates[j], ((0, 0), (0, LANES - n_gate))).astype(BF16)
            bg = jnp.pad(mlstm_b_gates[j], (0, LANES - n_gate)).reshape(1, LANES)
            q, k, v, xc, z, g, gt = _mlstm_features(
                rows_feat, u, mlstm_w_up[j].astype(BF16),
                mlstm_conv_w[j], mlstm_conv_b[j].reshape(1, inner),
                wq, wk, wv, wg, bg, k_scale=dh ** -0.5)
            hdir = _mlstm_scan(q, k, v, g, gt, batch=batch, seq=seq, ctx=ctx_len, chunk=chunk)
            a = _mlstm_out(rows, n_post, hdir, xc, z, mlstm_skip[j].reshape(1, inner),
                           mlstm_norm_w[j].reshape(1, inner))
            a_srcs = (a,)
            w_proj = mlstm_w_down[j].astype(BF16)
        (h,) = _half_ffn(rows, n_post, (h,), mod, nw, w_in_bf, w_out_bf, (i, 1),
                         fin_w, sub=2, emit="final" if last else "plain",
                         a_srcs=a_srcs, w_proj=w_proj)
        h_srcs = (h,)
    return h.reshape(batch, seq, d)
```

```python
import functools
import math

import jax
import jax.numpy as jnp
from jax import lax
from jax.experimental import pallas as pl
from jax.experimental.pallas import tpu as pltpu

N_MOD = 9
RMS_EPS = 1e-6
ATTN_HEADS = 8
ATTN_HEAD_DIM = 64
ATTN_V_DIM = 128
ROPE_THETA = 10000.0
GRID_W = 64
SUBLN_EPS = 1e-5
MLSTM_HEADS = 4
QKV_BLOCK = 4
CONV_W = 5
HEAD_LN_EPS = 1e-5
N_MIXERS = 2

LANES = 128
SUBLANES = 8
MXU_DIM = 256
MOD_ROWS_PAD = 8
ATTN_CHAIN_ROWS = 128
VMEM_LIMIT = 56 * 1024 * 1024

BF16 = jnp.bfloat16
F32 = jnp.float32


def _cparams(n_axes):
    return pltpu.CompilerParams(
        dimension_semantics=("arbitrary",) * n_axes,
        vmem_limit_bytes=VMEM_LIMIT)


def _resident(shape):
    nd = len(shape)
    return pl.BlockSpec(shape, lambda *_: (0,) * nd, pipeline_mode=pl.Buffered(1))


def _dot(a, b):
    return jnp.dot(a, b, preferred_element_type=F32)


def _dot_nt(a, b):
    return lax.dot_general(a, b, (((1,), (1,)), ((), ())), preferred_element_type=F32)


def _dot_tn(a, b):
    return lax.dot_general(a, b, (((0,), (0,)), ((), ())), preferred_element_type=F32)


def _sigmoid(x):
    return 1.0 / (1.0 + jnp.exp(-x))


def _silu(x):
    return x * _sigmoid(x)


class _Rows:
    def __init__(self, batch, seq, ctx, tile):
        assert seq % tile == 0 and (batch * ctx) % tile == 0
        self.batch, self.seq, self.ctx, self.tile = batch, seq, ctx, tile
        self.n_lat = batch * seq
        self.rows = batch * (seq + ctx)
        self.lat_tiles = self.n_lat // tile
        self.tiles = self.rows // tile
        self.tiles_per_seq = seq // tile

    def mod_row(self, i):
        return jnp.where(i < self.lat_tiles, i // self.tiles_per_seq, self.batch)

    def source_specs(self, n_src, width):
        tm = self.tile
        if n_src == 1:
            return [pl.BlockSpec((tm, width), lambda i: (i, 0))]
        lat_last = self.lat_tiles - 1
        return [pl.BlockSpec((tm, width), lambda i: (jnp.minimum(i, lat_last), 0)),
                pl.BlockSpec((tm, width), lambda i: (jnp.maximum(i - self.lat_tiles, 0), 0))]


def _pick_source(refs, lat_tiles):
    if len(refs) == 1:
        return refs[0][...]
    return jnp.where(pl.program_id(0) < lat_tiles, refs[0][...], refs[1][...])


def _mod_kernel(c_ref, w_ref, b_ref, o_ref):
    s = _silu(c_ref[...])
    o_ref[0] = jnp.dot(s, w_ref[0], preferred_element_type=F32,
                       precision=lax.Precision.HIGHEST) + b_ref[0]


def _modulation(cc, w_mod, b_mod):
    depth, d, nd = w_mod.shape
    rows = cc.shape[0]
    tn = d
    return pl.pallas_call(
        _mod_kernel,
        grid=(depth, nd // tn),
        in_specs=[pl.BlockSpec((rows, d), lambda l, j: (0, 0)),
                  pl.BlockSpec((1, d, tn), lambda l, j: (l, 0, j)),
                  pl.BlockSpec((1, 1, tn), lambda l, j: (l, 0, j))],
        out_specs=pl.BlockSpec((1, rows, tn), lambda l, j: (l, 0, j)),
        out_shape=jax.ShapeDtypeStruct((depth, rows, nd), F32),
        compiler_params=_cparams(2),
        name="modulation",
    )(cc, w_mod, b_mod.reshape(depth, 1, nd))


def _rms(x, w, eps):
    return x * lax.rsqrt(jnp.mean(x * x, axis=-1, keepdims=True) + eps) * w


def _ffn_chunks(d_ff):
    step = 3 * MXU_DIM
    out, c = [], 0
    while c < d_ff:
        w = min(step, d_ff - c)
        out.append((c, w))
        c += w
    return out


def _ffn_kernel(*refs, sub, d_ff, emit, n_h, n_a, lat_tiles):
    h_refs = refs[:n_h]
    mod_ref, nw_ref, win_ref, wout_ref, fin_ref = refs[n_h:n_h + 5]
    pos = n_h + 5
    a_refs = refs[pos:pos + n_a]
    pos += n_a
    if n_a:
        wp_ref = refs[pos]
        pos += 1
    out_refs = refs[pos:]

    x = _pick_source(h_refs, lat_tiles)
    if n_a:
        x = x + mod_ref[0, 5:6, :] * _dot(_pick_source(a_refs, lat_tiles), wp_ref[...])
    shift = mod_ref[0, 3 * sub:3 * sub + 1, :]
    scale = mod_ref[0, 3 * sub + 1:3 * sub + 2, :]
    gate = mod_ref[0, 3 * sub + 2:3 * sub + 3, :]
    u = (_rms(x, nw_ref[sub:sub + 1, :], RMS_EPS) * (1.0 + scale) + shift).astype(BF16)
    acc = jnp.zeros(x.shape, F32)
    for c0, cw in _ffn_chunks(d_ff):
        g = _dot(u, win_ref[:, c0:c0 + cw])
        up = _dot(u, win_ref[:, d_ff + c0:d_ff + c0 + cw])
        hid = (_silu(g) * up).astype(BF16)
        acc = acc + _dot(hid, wout_ref[c0:c0 + cw, :])
    y = x + 0.5 * gate * acc
    if emit == "mixer_in":
        out_refs[0][...] = y
        u2 = _rms(y, nw_ref[1:2, :], RMS_EPS) * (1.0 + mod_ref[0, 4:5, :]) + mod_ref[0, 3:4, :]
        out_refs[1][...] = u2.astype(BF16)
    elif emit == "final":
        out_refs[0][...] = _rms(y, fin_ref[...], RMS_EPS)
    else:
        out_refs[0][...] = y


def _stacked_weight(w, index):
    lead = len(index)
    zeros = (0,) * (w.ndim - lead)
    return pl.BlockSpec((None,) * lead + w.shape[lead:], lambda *_: tuple(index) + zeros,
                        pipeline_mode=pl.Buffered(1))


def _half_ffn(rows, n_tiles, h_srcs, mod, nw, w_in, w_out, w_index, fin_w, *, sub, emit,
              a_srcs=(), w_proj=None):
    d = h_srcs[0].shape[1]
    d_ff = w_out.shape[-2]
    tm = rows.tile
    row_spec = pl.BlockSpec((tm, d), lambda i: (i, 0))
    out_rows = n_tiles * tm
    out_shape = [jax.ShapeDtypeStruct((out_rows, d), F32)]
    out_specs = [row_spec]
    if emit == "mixer_in":
        out_shape.append(jax.ShapeDtypeStruct((out_rows, d), BF16))
        out_specs.append(row_spec)
    in_specs = (rows.source_specs(len(h_srcs), d)
                + [pl.BlockSpec((1, N_MOD, d), lambda i: (rows.mod_row(i), 0, 0)),
                   _resident(nw.shape), _stacked_weight(w_in, w_index),
                   _stacked_weight(w_out, w_index),
                   _resident(fin_w.shape)])
    args = list(h_srcs) + [mod, nw, w_in, w_out, fin_w]
    if a_srcs:
        in_specs += rows.source_specs(len(a_srcs), a_srcs[0].shape[1]) + [_resident(w_proj.shape)]
        args += list(a_srcs) + [w_proj]
    return pl.pallas_call(
        functools.partial(_ffn_kernel, sub=sub, d_ff=d_ff, emit=emit, n_h=len(h_srcs),
                          n_a=len(a_srcs), lat_tiles=rows.lat_tiles),
        grid=(n_tiles,),
        in_specs=in_specs,
        out_specs=out_specs,
        out_shape=out_shape,
        compiler_params=_cparams(1),
        name=f"half_ffn_{emit}_{len(h_srcs)}h{len(a_srcs)}a",
    )(*args)


def _rope_tables(seq, pad_rows):
    pos = jnp.arange(seq)
    row_pos = (pos // GRID_W).astype(F32)
    col_pos = (pos % GRID_W).astype(F32)
    half = ATTN_HEAD_DIM // 2
    inv_freq = 1.0 / (ROPE_THETA ** (jnp.arange(0, half, 2, dtype=F32) / half))
    ang = jnp.stack([row_pos[:, None] * inv_freq, col_pos[:, None] * inv_freq], axis=1)
    cos = jnp.cos(ang)
    sin = jnp.sin(ang)
    cos64 = jnp.stack([cos, cos], axis=2).reshape(seq, ATTN_HEAD_DIM)
    sin64 = jnp.stack([-sin, sin], axis=2).reshape(seq, ATTN_HEAD_DIM)
    cos128 = jnp.concatenate([cos64, cos64], axis=1)
    sin128 = jnp.concatenate([sin64, sin64], axis=1)
    cos128 = jnp.concatenate([cos128, jnp.ones((pad_rows, LANES), F32)], axis=0)
    sin128 = jnp.concatenate([sin128, jnp.zeros((pad_rows, LANES), F32)], axis=0)
    return cos128, sin128


def _rope(x, cos, sin, first_half):
    fwd = pltpu.roll(x, LANES - 16, axis=1)
    bwd = pltpu.roll(x, 16, axis=1)
    return x * cos + jnp.where(first_half, fwd, bwd) * sin


def _qkv_kernel(u_ref, w_ref, cos_ref, sin_ref, q_ref, k_ref, v_ref):
    u = u_ref[...]
    d = u.shape[1]
    cos = cos_ref[...]
    sin = sin_ref[...]
    lane = lax.broadcasted_iota(jnp.int32, cos.shape, 1)
    first_half = (lane % 32) < 16
    q = _dot(u, w_ref[:, 0:d])
    k = _dot(u, w_ref[:, d:2 * d])
    v = _dot(u, w_ref[:, 2 * d:3 * d]).astype(BF16)
    qk_scale = ATTN_HEAD_DIM ** -0.5 * math.log2(math.e)
    ones = jnp.ones((u.shape[0], LANES), BF16)
    for hd in range(d // LANES):
        sl = slice(hd * LANES, (hd + 1) * LANES)
        q_ref[:, sl] = (_rope(q[:, sl], cos, sin, first_half) * qk_scale).astype(BF16)
        k_ref[:, sl] = _rope(k[:, sl], cos, sin, first_half).astype(BF16)
        v_ref[:, 2 * hd * LANES:(2 * hd + 1) * LANES] = v[:, sl]
        v_ref[:, (2 * hd + 1) * LANES:(2 * hd + 2) * LANES] = ones


def _attn_qkv(rows, u, w_qkv, cos, sin):
    d = u.shape[1]
    tm = rows.tile
    row_spec = pl.BlockSpec((tm, d), lambda i: (i, 0))
    tab_spec = pl.BlockSpec(
        (tm, LANES),
        lambda i: (jnp.where(i < rows.lat_tiles, i % rows.tiles_per_seq, rows.tiles_per_seq), 0))
    return pl.pallas_call(
        _qkv_kernel,
        grid=(rows.tiles,),
        in_specs=[row_spec, _resident(w_qkv.shape), tab_spec, tab_spec],
        out_specs=[row_spec, row_spec, pl.BlockSpec((tm, 2 * d), lambda i: (i, 0))],
        out_shape=[jax.ShapeDtypeStruct(u.shape, BF16), jax.ShapeDtypeStruct(u.shape, BF16),
                   jax.ShapeDtypeStruct((u.shape[0], 2 * d), BF16)],
        compiler_params=_cparams(1),
        name="attn_qkv",
    )(u, w_qkv, cos, sin)


def _attn_rows(q_ref, o_ref, k_refs, v_refs, lam, subln, lambda_init, rows_per_chain):
    for r0 in range(0, q_ref.shape[0], rows_per_chain):
        q = q_ref[r0:r0 + rows_per_chain, :]
        lane = lax.broadcasted_iota(jnp.int32, q.shape, 1)
        zero = jnp.zeros_like(q)
        q_maps = (jnp.where(lane < ATTN_HEAD_DIM, q, zero), jnp.where(lane >= ATTN_HEAD_DIM, q, zero))
        outs = []
        for qm in q_maps:
            s = [_dot_nt(qm, k_ref[...]) for k_ref in k_refs]
            m = functools.reduce(jnp.maximum, [jnp.max(x, axis=-1, keepdims=True) for x in s])
            acc = None
            for x, v_ref in zip(s, v_refs):
                pv = _dot(jnp.exp2(x - m).astype(BF16), v_ref[...])
                acc = pv if acc is None else acc + pv
            outs.append(acc[:, 0:ATTN_V_DIM] / acc[:, ATTN_V_DIM:2 * ATTN_V_DIM])
        o = outs[0] - lam * outs[1]
        o_ref[r0:r0 + rows_per_chain, :] = (_rms(o, subln, SUBLN_EPS) * (1.0 - lambda_init)).astype(BF16)


def _attn_kernel(lam_ref, subln_ref, q_ref, qc_ref, kc_ref, kl_ref, vc_ref, vl_ref, o_ref, oc_ref,
                 *, lambda_init, rows_per_chain):
    lv = lam_ref[...]
    lam = (jnp.exp(jnp.sum(lv[0:1] * lv[1:2], axis=-1, keepdims=True))
           - jnp.exp(jnp.sum(lv[2:3] * lv[3:4], axis=-1, keepdims=True)) + lambda_init)
    subln = subln_ref[...]
    _attn_rows(q_ref, o_ref, (kc_ref, kl_ref), (vc_ref, vl_ref), lam, subln, lambda_init,
               rows_per_chain)
    _attn_rows(qc_ref, oc_ref, (kc_ref,), (vc_ref,), lam, subln, lambda_init,
               min(rows_per_chain, qc_ref.shape[0]))


def _attention(q, k, v, lam, subln_w, *, batch, seq, ctx, tq, lambda_init):
    d = q.shape[1]
    q_tiles = seq // tq
    n_lat = batch * seq
    assert n_lat % ctx == 0
    ctx_blk0 = n_lat // ctx
    lat_q = lambda b, h, t: (b * q_tiles + t, h)
    ctx_rows = lambda b, h, t: (ctx_blk0 + b, h)
    return pl.pallas_call(
        functools.partial(_attn_kernel, lambda_init=lambda_init,
                          rows_per_chain=min(tq, ATTN_CHAIN_ROWS)),
        grid=(batch, ATTN_HEADS, q_tiles),
        in_specs=[pl.BlockSpec(lam.shape, lambda b, h, t: (0, 0)),
                  pl.BlockSpec(subln_w.shape, lambda b, h, t: (0, 0)),
                  pl.BlockSpec((tq, LANES), lat_q),
                  pl.BlockSpec((ctx, LANES), ctx_rows),
                  pl.BlockSpec((ctx, LANES), ctx_rows),
                  pl.BlockSpec((seq, LANES), lambda b, h, t: (b, h)),
                  pl.BlockSpec((ctx, 2 * LANES), ctx_rows),
                  pl.BlockSpec((seq, 2 * LANES), lambda b, h, t: (b, h))],
        out_specs=[pl.BlockSpec((tq, LANES), lat_q),
                   pl.BlockSpec((ctx, LANES), lambda b, h, t: (b, h))],
        out_shape=[jax.ShapeDtypeStruct((n_lat, d), BF16),
                   jax.ShapeDtypeStruct((batch * ctx, d), BF16)],
        compiler_params=_cparams(3),
        name="diff_attention",
    )(lam, subln_w, q, q, k, k, v, v)


def _log_sigmoid(x):
    return jnp.minimum(x, 0.0) - jnp.log1p(jnp.exp(-jnp.abs(x)))


HALO = 16


def _feat_kernel(u_ref, up_ref, un_ref, wup_ref, cw_ref, cb_ref, wq_ref, wk_ref, wv_ref, wg_ref, bg_ref,
                 q_ref, k_ref, v_ref, xc_ref, z_ref, g_ref, gt_ref, xe_ref,
                 *, tf, lat_tiles, lat_tps, ctx_tps, k_scale):
    i = pl.program_id(0)
    j = jnp.where(i < lat_tiles, i % lat_tps, (i - lat_tiles) % ctx_tps)
    tps = jnp.where(i < lat_tiles, lat_tps, ctx_tps)
    first = j == 0
    last = j == tps - 1
    halo = HALO
    inner = xc_ref.shape[1]
    u = u_ref[...]
    u_ext = jnp.concatenate([up_ref[...], u, un_ref[...]], axis=0)
    x_ext = _dot(u_ext, wup_ref[:, 0:inner])
    z_ref[...] = _dot(u, wup_ref[:, inner:2 * inner]).astype(BF16)
    x = x_ext[halo:halo + tf, :]
    xe_ref[0:halo, :] = jnp.where(first, 0.0, x_ext[0:halo, :])
    xe_ref[halo:halo + tf, :] = x
    xe_ref[halo + tf:2 * halo + tf, :] = jnp.where(last, 0.0, x_ext[halo + tf:2 * halo + tf, :])
    conv = jnp.zeros(x.shape, F32) + cb_ref[...]
    for t in range(CONV_W):
        off = halo - CONV_W // 2 + t
        conv = conv + cw_ref[t:t + 1, :] * xe_ref[off:off + tf, :]
    xc = _silu(conv)
    xc_ref[...] = xc.astype(BF16)
    xc_b = xc.astype(BF16)
    x_b = x.astype(BF16)
    nblk = wq_ref.shape[0]
    bw = wq_ref.shape[1]
    gates = jnp.zeros((tf, LANES), F32) + bg_ref[...]
    for part, (src, w_ref, o_ref, scl) in enumerate(
            ((xc_b, wq_ref, q_ref, 1.0), (xc_b, wk_ref, k_ref, k_scale), (x_b, wv_ref, v_ref, 1.0))):
        for jb in range(nblk):
            sl = slice(jb * bw, (jb + 1) * bw)
            y = _dot(src[:, sl], w_ref[jb])
            yb = y.astype(BF16)
            o_ref[:, sl] = yb if scl == 1.0 else (y * scl).astype(BF16)
            gates = gates + _dot(yb, wg_ref[part * inner + jb * bw:part * inner + (jb + 1) * bw, :])
    lane = lax.broadcasted_iota(jnp.int32, gates.shape, 1)
    is_forget = (lane % (2 * MLSTM_HEADS)) >= MLSTM_HEADS
    gates = jnp.where(is_forget, _log_sigmoid(gates), gates)
    g1 = pltpu.roll(gates, LANES - 2 * MLSTM_HEADS, axis=1)
    g_ref[0] = gates
    g_ref[1] = g1
    gt_ref[0] = gates.T[0:2 * SUBLANES, :]
    gt_ref[1] = g1.T[0:2 * SUBLANES, :]


def _mlstm_features(rows_feat, u, w_up, conv_w, conv_b, wq, wk, wv, wg, bg, *, k_scale):
    r, d = u.shape
    inner = w_up.shape[1] // 2
    tf = rows_feat.tile
    halo = HALO
    nh = r // halo
    u_main = pl.BlockSpec((tf, d), lambda i: (i, 0))
    u_prev = pl.BlockSpec((halo, d), lambda i: (jnp.maximum(i * (tf // halo) - 1, 0), 0))
    u_next = pl.BlockSpec((halo, d), lambda i: (jnp.minimum((i + 1) * (tf // halo), nh - 1), 0))
    main = pl.BlockSpec((tf, inner), lambda i: (i, 0))
    bf_out = jax.ShapeDtypeStruct((r, inner), BF16)
    return pl.pallas_call(
        functools.partial(_feat_kernel, tf=tf, lat_tiles=rows_feat.lat_tiles,
                          lat_tps=rows_feat.tiles_per_seq, ctx_tps=rows_feat.ctx // tf,
                          k_scale=k_scale),
        grid=(rows_feat.tiles,),
        in_specs=[u_main, u_prev, u_next, _resident(w_up.shape),
                  _resident(conv_w.shape), _resident(conv_b.shape),
                  _resident(wq.shape), _resident(wk.shape), _resident(wv.shape),
                  _resident(wg.shape), _resident(bg.shape)],
        out_specs=[main, main, main, main, main,
                   pl.BlockSpec((2, tf, LANES), lambda i: (0, i, 0)),
                   pl.BlockSpec((2, 2 * SUBLANES, tf), lambda i: (0, 0, i))],
        out_shape=[bf_out, bf_out, bf_out, bf_out, bf_out,
                   jax.ShapeDtypeStruct((2, r, LANES), F32),
                   jax.ShapeDtypeStruct((2, 2 * SUBLANES, r), F32)],
        scratch_shapes=[pltpu.VMEM((tf + 2 * halo, inner), F32)],
        compiler_params=_cparams(1),
        name="mlstm_features",
    )(u, u, u, w_up, conv_w, conv_b, wq, wk, wv, wg, bg)


def _split3(x):
    a = x.astype(BF16)
    r = x - a.astype(F32)
    b = r.astype(BF16)
    c = (r - b.astype(F32)).astype(BF16)
    return a, b, c


def _scan_kernel(q_ref, k_ref, v_ref, g_ref, gt_ref, h_ref, c_scr, cb_scr, n_scr, m_scr,
                 *, chunk, dh):
    d = pl.program_id(1)
    p = pl.program_id(2)

    @pl.when(p == 0)
    def _():
        c_scr[...] = jnp.zeros(c_scr.shape, F32)
        cb_scr[...] = jnp.zeros(cb_scr.shape, BF16)
        n_scr[...] = jnp.zeros(n_scr.shape, F32)
        m_scr[...] = jnp.zeros(m_scr.shape, F32)

    g = g_ref[0]
    gt = gt_ref[0]
    ti = lax.broadcasted_iota(jnp.int32, (chunk, chunk), 0)
    si = lax.broadcasted_iota(jnp.int32, (chunk, chunk), 1)
    sgn = 1 - 2 * d
    vis = (ti - si) * sgn >= 0
    tri = jnp.where(vis, 1.0, 0.0).astype(BF16)
    tri_t = jnp.where((si - ti) * sgn >= 0, 1.0, 0.0).astype(BF16)
    bc_all = functools.reduce(lambda a, b: a + b, [_dot(tri, t) for t in _split3(g)])
    br_all = functools.reduce(lambda a, b: a + b, [_dot(t, tri_t) for t in _split3(gt)])
    neg_inf = jnp.float32(-jnp.inf)
    for hd in range(MLSTM_HEADS):
        sl = slice(hd * dh, (hd + 1) * dh)
        q = q_ref[:, sl]
        k = k_ref[:, sl]
        v = v_ref[:, sl]
        ic = g[:, hd:hd + 1]
        fc = g[:, MLSTM_HEADS + hd:MLSTM_HEADS + hd + 1]
        bc = bc_all[:, MLSTM_HEADS + hd:MLSTM_HEADS + hd + 1]
        ir = gt[hd:hd + 1, :]
        br = br_all[MLSTM_HEADS + hd:MLSTM_HEADS + hd + 1, :]
        m_prev = m_scr[hd]
        n_vec = n_scr[hd]
        log_d = jnp.where(vis, bc - br + ir, neg_inf)
        m_inter = bc + m_prev
        m_t = jnp.maximum(m_inter, jnp.max(log_d, axis=-1, keepdims=True))
        s = _dot_nt(q, k) * jnp.exp(log_d - m_t)
        w_inter = jnp.exp(m_inter - m_t)
        qf = q.astype(F32)
        num = _dot(s.astype(BF16), v) + w_inter * _dot_nt(q, cb_scr[hd])
        den = jnp.sum(s, axis=-1, keepdims=True) + w_inter * jnp.sum(qf * n_vec, axis=-1, keepdims=True)
        h_ref[0, :, sl] = (num / jnp.maximum(jnp.abs(den), jnp.exp(-m_t))).astype(h_ref.dtype)
        b_tot = jnp.sum(fc, axis=0, keepdims=True)
        log_w = b_tot - bc + ic
        m_new = jnp.maximum(b_tot + m_prev, jnp.max(log_w, axis=0, keepdims=True))
        w_s = jnp.exp(log_w - m_new)
        decay = jnp.exp(b_tot + m_prev - m_new)
        vw = (v.astype(F32) * w_s).astype(BF16)
        c_new = decay * c_scr[hd] + _dot_tn(vw, k)
        c_scr[hd] = c_new
        cb_scr[hd] = c_new.astype(BF16)
        n_scr[hd] = decay * n_vec + jnp.sum(k.astype(F32) * w_s, axis=0, keepdims=True)
        m_scr[hd] = m_new


def _mlstm_scan(q, k, v, g, gt, *, batch, seq, ctx, chunk):
    r, inner = q.shape
    dh = inner // MLSTM_HEADS
    ncc = ctx // chunk
    ncl = seq // chunk
    lat_blocks = batch * ncl

    def blk(b, d, p):
        jc = jnp.where(d == 0, p, ncc - 1 - p)
        jl = jnp.where(d == 0, p - ncc, ncl - 1 - (p - ncc))
        return jnp.where(p < ncc, lat_blocks + b * ncc + jc, b * ncl + jl)

    row_spec = pl.BlockSpec((chunk, inner), lambda b, d, p: (blk(b, d, p), 0))
    return pl.pallas_call(
        functools.partial(_scan_kernel, chunk=chunk, dh=dh),
        grid=(batch, 2, ncc + ncl),
        in_specs=[row_spec, row_spec, row_spec,
                  pl.BlockSpec((1, chunk, LANES), lambda b, d, p: (d, blk(b, d, p), 0)),
                  pl.BlockSpec((1, 2 * SUBLANES, chunk), lambda b, d, p: (d, 0, blk(b, d, p)))],
        out_specs=pl.BlockSpec((1, chunk, inner), lambda b, d, p: (d, blk(b, d, p), 0)),
        out_shape=jax.ShapeDtypeStruct((2, r, inner), BF16),
        scratch_shapes=[pltpu.VMEM((MLSTM_HEADS, dh, dh), F32),
                        pltpu.VMEM((MLSTM_HEADS, dh, dh), BF16),
                        pltpu.VMEM((MLSTM_HEADS, 1, dh), F32),
                        pltpu.VMEM((MLSTM_HEADS, 1, 1), F32)],
        compiler_params=_cparams(3),
        name="mlstm_scan",
    )(q, k, v, g, gt)


def _mlstm_out_kernel(hd_ref, xc_ref, z_ref, skip_ref, nw_ref, a_ref, *, dh):
    for hd in range(MLSTM_HEADS):
        sl = slice(hd * dh, (hd + 1) * dh)
        hh = hd_ref[0, :, sl].astype(F32) + hd_ref[1, :, sl].astype(F32)
        mu = jnp.mean(hh, axis=-1, keepdims=True)
        cen = hh - mu
        var = jnp.mean(cen * cen, axis=-1, keepdims=True)
        hn = cen * lax.rsqrt(var + HEAD_LN_EPS) * nw_ref[:, sl]
        y = (hn + skip_ref[:, sl] * xc_ref[:, sl].astype(F32)) * _silu(z_ref[:, sl].astype(F32))
        a_ref[:, sl] = y.astype(BF16)


def _mlstm_out(rows, n_tiles, hdir, xc, z, skip, norm_w):
    inner = xc.shape[1]
    tm = rows.tile
    row_spec = pl.BlockSpec((tm, inner), lambda i: (i, 0))
    return pl.pallas_call(
        functools.partial(_mlstm_out_kernel, dh=inner // MLSTM_HEADS),
        grid=(n_tiles,),
        in_specs=[pl.BlockSpec((2, tm, inner), lambda i: (0, i, 0)), row_spec, row_spec,
                  _resident(skip.shape), _resident(norm_w.shape)],
        out_specs=row_spec,
        out_shape=jax.ShapeDtypeStruct((n_tiles * tm, inner), BF16),
        compiler_params=_cparams(1),
        name="mlstm_out",
    )(hdir, xc, z, skip, norm_w)


def _blockdiag_dense(w, block):
    g = w.shape[0]
    per = block // QKV_BLOCK
    rows = w.reshape(g // per, block, QKV_BLOCK)
    tiled = jnp.tile(rows, (1, 1, per))
    r = lax.broadcasted_iota(jnp.int32, (block, block), 0) // QKV_BLOCK
    c = lax.broadcasted_iota(jnp.int32, (block, block), 1) // QKV_BLOCK
    return jnp.where(r == c, tiled, jnp.zeros_like(tiled))


def _pick_tile(seq, ctx_rows, target):
    t = target
    while seq % t or ctx_rows % t:
        t //= 2
    return t


def kernel(x, c, ctx, c_ctx, w_mod, b_mod, norm_w, ffn_w_in, ffn_w_out,
           attn_w_qkv, attn_w_o, attn_lambda, attn_subln_w,
           mlstm_w_up, mlstm_conv_w, mlstm_conv_b, mlstm_w_qkv, mlstm_w_gates, mlstm_b_gates,
           mlstm_skip, mlstm_norm_w, mlstm_w_down, final_norm_w):
    batch, seq, d = x.shape
    ctx_len = ctx.shape[1]
    depth = w_mod.shape[0]
    inner = mlstm_w_down.shape[1]
    dh = inner // MLSTM_HEADS

    rows = _Rows(batch, seq, ctx_len, _pick_tile(seq, batch * ctx_len, 512))
    rows_feat = _Rows(batch, seq, ctx_len, _pick_tile(seq, ctx_len, 256))
    chunk = _pick_tile(seq, ctx_len, 256)
    tq = _pick_tile(seq, seq, 2048)

    n_mod_rows = -(-(batch + 1) // MOD_ROWS_PAD) * MOD_ROWS_PAD
    cc = jnp.concatenate([c, c_ctx[None, :], jnp.zeros((n_mod_rows - batch - 1, d), F32)], axis=0)
    mod_all = _modulation(cc, w_mod, b_mod).reshape(depth, n_mod_rows, N_MOD, d)

    cos, sin = _rope_tables(seq, rows.tile)
    fin_w = final_norm_w.reshape(1, d)
    h_srcs = (x.reshape(batch * seq, d), ctx.reshape(batch * ctx_len, d))
    w_in_bf = ffn_w_in.astype(BF16)
    w_out_bf = ffn_w_out.astype(BF16)

    for i in range(depth):
        last = i == depth - 1
        j = i // N_MIXERS
        mod = mod_all[i]
        nw = norm_w[i]
        h, u = _half_ffn(rows, rows.tiles, h_srcs, mod, nw, w_in_bf, w_out_bf, (i, 0),
                         fin_w, sub=0, emit="mixer_in")
        n_post = rows.lat_tiles if last else rows.tiles
        if i % N_MIXERS == 0:
            lambda_init = 0.8 - 0.6 * math.exp(-0.3 * i)
            q, k, v = _attn_qkv(rows, u, attn_w_qkv[j].astype(BF16), cos, sin)
            lam = attn_lambda[j]
            subln = attn_subln_w[j].reshape(1, ATTN_V_DIM)
            a_srcs = tuple(_attention(q, k, v, lam, subln, batch=batch, seq=seq, ctx=ctx_len, tq=tq,
                                      lambda_init=lambda_init))
            w_proj = attn_w_o[j].astype(BF16)
        else:
            wq = _blockdiag_dense(mlstm_w_qkv[j, 0], MXU_DIM).astype(BF16)
            wk = _blockdiag_dense(mlstm_w_qkv[j, 1], MXU_DIM).astype(BF16)
            wv = _blockdiag_dense(mlstm_w_qkv[j, 2], MXU_DIM).astype(BF16)
            n_gate = mlstm_w_gates.shape[2]
            wg = jnp.pad(mlstm_w_gates[j], ((0, 0), (0, LANES - n_gate))).astype(BF16)
            bg = jnp.pad(mlstm_b_gates[j], (0, LANES - n_gate)).reshape(1, LANES)
            q, k, v, xc, z, g, gt = _mlstm_features(
                rows_feat, u, mlstm_w_up[j].astype(BF16),
                mlstm_conv_w[j], mlstm_conv_b[j].reshape(1, inner),
                wq, wk, wv, wg, bg, k_scale=dh ** -0.5)
            hdir = _mlstm_scan(q, k, v, g, gt, batch=batch, seq=seq, ctx=ctx_len, chunk=chunk)
            a = _mlstm_out(rows, n_post, hdir, xc, z, mlstm_skip[j].reshape(1, inner),
                           mlstm_norm_w[j].reshape(1, inner))
            a_srcs = (a,)
            w_proj = mlstm_w_down[j].astype(BF16)
        (h,) = _half_ffn(rows, n_post, (h,), mod, nw, w_in_bf, w_out_bf, (i, 1),
                         fin_w, sub=2, emit="final" if last else "plain",
                         a_srcs=a_srcs, w_proj=w_proj)
        h_srcs = (h,)
    return h.reshape(batch, seq, d)
```

```python
import functools
import math

import jax
import jax.numpy as jnp
from jax import lax
from jax.experimental import pallas as pl
from jax.experimental.pallas import tpu as pltpu

N_MOD = 9
RMS_EPS = 1e-6
ATTN_HEADS = 8
ATTN_HEAD_DIM = 64
ATTN_V_DIM = 128
ROPE_THETA = 10000.0
GRID_W = 64
SUBLN_EPS = 1e-5
MLSTM_HEADS = 4
QKV_BLOCK = 4
CONV_W = 5
HEAD_LN_EPS = 1e-5
N_MIXERS = 2

LANES = 128
SUBLANES = 8
MXU_DIM = 256
MOD_ROWS_PAD = 8
ATTN_CHAIN_ROWS = 128
VMEM_LIMIT = 56 * 1024 * 1024

BF16 = jnp.bfloat16
F32 = jnp.float32


def _cparams(n_axes):
    return pltpu.CompilerParams(
        dimension_semantics=("arbitrary",) * n_axes,
        vmem_limit_bytes=VMEM_LIMIT)


def _resident(shape):
    nd = len(shape)
    return pl.BlockSpec(shape, lambda *_: (0,) * nd, pipeline_mode=pl.Buffered(1))


def _dot(a, b):
    return jnp.dot(a, b, preferred_element_type=F32)


def _dot_nt(a, b):
    return lax.dot_general(a, b, (((1,), (1,)), ((), ())), preferred_element_type=F32)


def _dot_tn(a, b):
    return lax.dot_general(a, b, (((0,), (0,)), ((), ())), preferred_element_type=F32)


def _sigmoid(x):
    return 1.0 / (1.0 + jnp.exp(-x))


def _silu(x):
    return x * _sigmoid(x)


class _Rows:
    def __init__(self, batch, seq, ctx, tile):
        assert seq % tile == 0 and (batch * ctx) % tile == 0
        self.batch, self.seq, self.ctx, self.tile = batch, seq, ctx, tile
        self.n_lat = batch * seq
        self.rows = batch * (seq + ctx)
        self.lat_tiles = self.n_lat // tile
        self.tiles = self.rows // tile
        self.tiles_per_seq = seq // tile

    def mod_row(self, i):
        return jnp.where(i < self.lat_tiles, i // self.tiles_per_seq, self.batch)

    def source_specs(self, n_src, width):
        tm = self.tile
        if n_src == 1:
            return [pl.BlockSpec((tm, width), lambda i: (i, 0))]
        lat_last = self.lat_tiles - 1
        return [pl.BlockSpec((tm, width), lambda i: (jnp.minimum(i, lat_last), 0)),
                pl.BlockSpec((tm, width), lambda i: (jnp.maximum(i - self.lat_tiles, 0), 0))]


def _pick_source(refs, lat_tiles):
    if len(refs) == 1:
        return refs[0][...]
    return jnp.where(pl.program_id(0) < lat_tiles, refs[0][...], refs[1][...])


def _mod_kernel(c_ref, w_ref, b_ref, o_ref):
    s = _silu(c_ref[...])
    o_ref[0] = jnp.dot(s, w_ref[0], preferred_element_type=F32,
                       precision=lax.Precision.HIGHEST) + b_ref[0]


def _modulation(cc, w_mod, b_mod):
    depth, d, nd = w_mod.shape
    rows = cc.shape[0]
    tn = d
    return pl.pallas_call(
        _mod_kernel,
        grid=(depth, nd // tn),
        in_specs=[pl.BlockSpec((rows, d), lambda l, j: (0, 0)),
                  pl.BlockSpec((1, d, tn), lambda l, j: (l, 0, j)),
                  pl.BlockSpec((1, 1, tn), lambda l, j: (l, 0, j))],
        out_specs=pl.BlockSpec((1, rows, tn), lambda l, j: (l, 0, j)),
        out_shape=jax.ShapeDtypeStruct((depth, rows, nd), F32),
        compiler_params=_cparams(2),
        name="modulation",
    )(cc, w_mod, b_mod.reshape(depth, 1, nd))


def _rms(x, w, eps):
    return x * lax.rsqrt(jnp.mean(x * x, axis=-1, keepdims=True) + eps) * w


def _ffn_chunks(d_ff):
    step = 3 * MXU_DIM
    out, c = [], 0
    while c < d_ff:
        w = min(step, d_ff - c)
        out.append((c, w))
        c += w
    return out


def _ffn_kernel(*refs, sub, d_ff, emit, n_h, n_a, lat_tiles):
    h_refs = refs[:n_h]
    mod_ref, nw_ref, win_ref, wout_ref, fin_ref = refs[n_h:n_h + 5]
    pos = n_h + 5
    a_refs = refs[pos:pos + n_a]
    pos += n_a
    if n_a:
        wp_ref = refs[pos]
        pos += 1
    out_refs = refs[pos:]

    x = _pick_source(h_refs, lat_tiles)
    if n_a:
        x = x + mod_ref[0, 5:6, :] * _dot(_pick_source(a_refs, lat_tiles), wp_ref[...])
    shift = mod_ref[0, 3 * sub:3 * sub + 1, :]
    scale = mod_ref[0, 3 * sub + 1:3 * sub + 2, :]
    gate = mod_ref[0, 3 * sub + 2:3 * sub + 3, :]
    u = (_rms(x, nw_ref[sub:sub + 1, :], RMS_EPS) * (1.0 + scale) + shift).astype(BF16)
    acc = jnp.zeros(x.shape, F32)
    for c0, cw in _ffn_chunks(d_ff):
        g = _dot(u, win_ref[:, c0:c0 + cw])
        up = _dot(u, win_ref[:, d_ff + c0:d_ff + c0 + cw])
        hid = (_silu(g) * up).astype(BF16)
        acc = acc + _dot(hid, wout_ref[c0:c0 + cw, :])
    y = x + 0.5 * gate * acc
    if emit == "mixer_in":
        out_refs[0][...] = y
        u2 = _rms(y, nw_ref[1:2, :], RMS_EPS) * (1.0 + mod_ref[0, 4:5, :]) + mod_ref[0, 3:4, :]
        out_refs[1][...] = u2.astype(BF16)
    elif emit == "final":
        out_refs[0][...] = _rms(y, fin_ref[...], RMS_EPS)
    else:
        out_refs[0][...] = y


def _stacked_weight(w, index):
    lead = len(index)
    zeros = (0,) * (w.ndim - lead)
    return pl.BlockSpec((None,) * lead + w.shape[lead:], lambda *_: tuple(index) + zeros,
                        pipeline_mode=pl.Buffered(1))


def _half_ffn(rows, n_tiles, h_srcs, mod, nw, w_in, w_out, w_index, fin_w, *, sub, emit,
              a_srcs=(), w_proj=None):
    d = h_srcs[0].shape[1]
    d_ff = w_out.shape[-2]
    tm = rows.tile
    row_spec = pl.BlockSpec((tm, d), lambda i: (i, 0))
    out_rows = n_tiles * tm
    out_shape = [jax.ShapeDtypeStruct((out_rows, d), F32)]
    out_specs = [row_spec]
    if emit == "mixer_in":
        out_shape.append(jax.ShapeDtypeStruct((out_rows, d), BF16))
        out_specs.append(row_spec)
    in_specs = (rows.source_specs(len(h_srcs), d)
                + [pl.BlockSpec((1, N_MOD, d), lambda i: (rows.mod_row(i), 0, 0)),
                   _resident(nw.shape), _stacked_weight(w_in, w_index),
                   _stacked_weight(w_out, w_index),
                   _resident(fin_w.shape)])
    args = list(h_srcs) + [mod, nw, w_in, w_out, fin_w]
    if a_srcs:
        in_specs += rows.source_specs(len(a_srcs), a_srcs[0].shape[1]) + [_resident(w_proj.shape)]
        args += list(a_srcs) + [w_proj]
    return pl.pallas_call(
        functools.partial(_ffn_kernel, sub=sub, d_ff=d_ff, emit=emit, n_h=len(h_srcs),
                          n_a=len(a_srcs), lat_tiles=rows.lat_tiles),
        grid=(n_tiles,),
        in_specs=in_specs,
        out_specs=out_specs,
        out_shape=out_shape,
        compiler_params=_cparams(1),
        name=f"half_ffn_{emit}_{len(h_srcs)}h{len(a_srcs)}a",
    )(*args)


def _rope_tables(seq, pad_rows):
    pos = jnp.arange(seq)
    row_pos = (pos // GRID_W).astype(F32)
    col_pos = (pos % GRID_W).astype(F32)
    half = ATTN_HEAD_DIM // 2
    inv_freq = 1.0 / (ROPE_THETA ** (jnp.arange(0, half, 2, dtype=F32) / half))
    ang = jnp.stack([row_pos[:, None] * inv_freq, col_pos[:, None] * inv_freq], axis=1)
    cos = jnp.cos(ang)
    sin = jnp.sin(ang)
    cos64 = jnp.stack([cos, cos], axis=2).reshape(seq, ATTN_HEAD_DIM)
    sin64 = jnp.stack([-sin, sin], axis=2).reshape(seq, ATTN_HEAD_DIM)
    cos128 = jnp.concatenate([cos64, cos64], axis=1)
    sin128 = jnp.concatenate([sin64, sin64], axis=1)
    cos128 = jnp.concatenate([cos128, jnp.ones((pad_rows, LANES), F32)], axis=0)
    sin128 = jnp.concatenate([sin128, jnp.zeros((pad_rows, LANES), F32)], axis=0)
    return cos128, sin128


def _rope(x, cos, sin, first_half):
    fwd = pltpu.roll(x, LANES - 16, axis=1)
    bwd = pltpu.roll(x, 16, axis=1)
    return x * cos + jnp.where(first_half, fwd, bwd) * sin


def _qkv_kernel(u_ref, w_ref, cos_ref, sin_ref, q_ref, k_ref, v_ref):
    u = u_ref[...]
    d = u.shape[1]
    cos = cos_ref[...]
    sin = sin_ref[...]
    lane = lax.broadcasted_iota(jnp.int32, cos.shape, 1)
    first_half = (lane % 32) < 16
    q = _dot(u, w_ref[:, 0:d])
    k = _dot(u, w_ref[:, d:2 * d])
    v = _dot(u, w_ref[:, 2 * d:3 * d]).astype(BF16)
    qk_scale = ATTN_HEAD_DIM ** -0.5 * math.log2(math.e)
    ones = jnp.ones((u.shape[0], LANES), BF16)
    for hd in range(d // LANES):
        sl = slice(hd * LANES, (hd + 1) * LANES)
        q_ref[:, sl] = (_rope(q[:, sl], cos, sin, first_half) * qk_scale).astype(BF16)
        k_ref[:, sl] = _rope(k[:, sl], cos, sin, first_half).astype(BF16)
        v_ref[:, 2 * hd * LANES:(2 * hd + 1) * LANES] = v[:, sl]
        v_ref[:, (2 * hd + 1) * LANES:(2 * hd + 2) * LANES] = ones


def _attn_qkv(rows, u, w_qkv, cos, sin):
    d = u.shape[1]
    tm = rows.tile
    row_spec = pl.BlockSpec((tm, d), lambda i: (i, 0))
    tab_spec = pl.BlockSpec(
        (tm, LANES),
        lambda i: (jnp.where(i < rows.lat_tiles, i % rows.tiles_per_seq, rows.tiles_per_seq), 0))
    return pl.pallas_call(
        _qkv_kernel,
        grid=(rows.tiles,),
        in_specs=[row_spec, _resident(w_qkv.shape), tab_spec, tab_spec],
        out_specs=[row_spec, row_spec, pl.BlockSpec((tm, 2 * d), lambda i: (i, 0))],
        out_shape=[jax.ShapeDtypeStruct(u.shape, BF16), jax.ShapeDtypeStruct(u.shape, BF16),
                   jax.ShapeDtypeStruct((u.shape[0], 2 * d), BF16)],
        compiler_params=_cparams(1),
        name="attn_qkv",
    )(u, w_qkv, cos, sin)


def _attn_rows(q_ref, o_ref, k_refs, v_refs, lam, subln, lambda_init, rows_per_chain):
    for r0 in range(0, q_ref.shape[0], rows_per_chain):
        q = q_ref[r0:r0 + rows_per_chain, :]
        lane = lax.broadcasted_iota(jnp.int32, q.shape, 1)
        zero = jnp.zeros_like(q)
        q_maps = (jnp.where(lane < ATTN_HEAD_DIM, q, zero), jnp.where(lane >= ATTN_HEAD_DIM, q, zero))
        outs = []
        for qm in q_maps:
            s = [_dot_nt(qm, k_ref[...]) for k_ref in k_refs]
            m = functools.reduce(jnp.maximum, [jnp.max(x, axis=-1, keepdims=True) for x in s])
            acc = None
            for x, v_ref in zip(s, v_refs):
                pv = _dot(jnp.exp2(x - m).astype(BF16), v_ref[...])
                acc = pv if acc is None else acc + pv
            outs.append(acc[:, 0:ATTN_V_DIM] / acc[:, ATTN_V_DIM:2 * ATTN_V_DIM])
        o = outs[0] - lam * outs[1]
        o_ref[r0:r0 + rows_per_chain, :] = (_rms(o, subln, SUBLN_EPS) * (1.0 - lambda_init)).astype(BF16)


def _attn_kernel(lam_ref, subln_ref, q_ref, qc_ref, kc_ref, kl_ref, vc_ref, vl_ref, o_ref, oc_ref,
                 *, lambda_init, rows_per_chain):
    lv = lam_ref[...]
    lam = (jnp.exp(jnp.sum(lv[0:1] * lv[1:2], axis=-1, keepdims=True))
           - jnp.exp(jnp.sum(lv[2:3] * lv[3:4], axis=-1, keepdims=True)) + lambda_init)
    subln = subln_ref[...]
    _attn_rows(q_ref, o_ref, (kc_ref, kl_ref), (vc_ref, vl_ref), lam, subln, lambda_init,
               rows_per_chain)
    _attn_rows(qc_ref, oc_ref, (kc_ref,), (vc_ref,), lam, subln, lambda_init,
               min(rows_per_chain, qc_ref.shape[0]))


def _attention(q, k, v, lam, subln_w, *, batch, seq, ctx, tq, lambda_init):
    d = q.shape[1]
    q_tiles = seq // tq
    n_lat = batch * seq
    assert n_lat % ctx == 0
    ctx_blk0 = n_lat // ctx
    lat_q = lambda b, h, t: (b * q_tiles + t, h)
    ctx_rows = lambda b, h, t: (ctx_blk0 + b, h)
    return pl.pallas_call(
        functools.partial(_attn_kernel, lambda_init=lambda_init,
                          rows_per_chain=min(tq, ATTN_CHAIN_ROWS)),
        grid=(batch, ATTN_HEADS, q_tiles),
        in_specs=[pl.BlockSpec(lam.shape, lambda b, h, t: (0, 0)),
                  pl.BlockSpec(subln_w.shape, lambda b, h, t: (0, 0)),
                  pl.BlockSpec((tq, LANES), lat_q),
                  pl.BlockSpec((ctx, LANES), ctx_rows),
                  pl.BlockSpec((ctx, LANES), ctx_rows),
                  pl.BlockSpec((seq, LANES), lambda b, h, t: (b, h)),
                  pl.BlockSpec((ctx, 2 * LANES), ctx_rows),
                  pl.BlockSpec((seq, 2 * LANES), lambda b, h, t: (b, h))],
        out_specs=[pl.BlockSpec((tq, LANES), lat_q),
                   pl.BlockSpec((ctx, LANES), lambda b, h, t: (b, h))],
        out_shape=[jax.ShapeDtypeStruct((n_lat, d), BF16),
                   jax.ShapeDtypeStruct((batch * ctx, d), BF16)],
        compiler_params=_cparams(3),
        name="diff_attention",
    )(lam, subln_w, q, q, k, k, v, v)


def _log_sigmoid(x):
    return jnp.minimum(x, 0.0) - jnp.log1p(jnp.exp(-jnp.abs(x)))


HALO = 16


def _feat_kernel(u_ref, up_ref, un_ref, wup_ref, cw_ref, cb_ref, wq_ref, wk_ref, wv_ref, wg_ref, bg_ref,
                 q_ref, k_ref, v_ref, xc_ref, z_ref, g_ref, gt_ref, xe_ref,
                 *, tf, lat_tiles, lat_tps, ctx_tps, k_scale):
    i = pl.program_id(0)
    j = jnp.where(i < lat_tiles, i % lat_tps, (i - lat_tiles) % ctx_tps)
    tps = jnp.where(i < lat_tiles, lat_tps, ctx_tps)
    first = j == 0
    last = j == tps - 1
    halo = HALO
    inner = xc_ref.shape[1]
    u = u_ref[...]
    u_ext = jnp.concatenate([up_ref[...], u, un_ref[...]], axis=0)
    x_ext = _dot(u_ext, wup_ref[:, 0:inner])
    z_ref[...] = _dot(u, wup_ref[:, inner:2 * inner]).astype(BF16)
    x = x_ext[halo:halo + tf, :]
    xe_ref[0:halo, :] = jnp.where(first, 0.0, x_ext[0:halo, :])
    xe_ref[halo:halo + tf, :] = x
    xe_ref[halo + tf:2 * halo + tf, :] = jnp.where(last, 0.0, x_ext[halo + tf:2 * halo + tf, :])
    conv = jnp.zeros(x.shape, F32) + cb_ref[...]
    xe = xe_ref[...]
    n_ext = tf + 2 * halo
    for t in range(CONV_W):
        win = xe if t == CONV_W // 2 else pltpu.roll(xe, (CONV_W // 2 - t) % n_ext, axis=0)
        conv = conv + cw_ref[t:t + 1, :] * win[halo:halo + tf, :]
    xc = _silu(conv)
    xc_ref[...] = xc.astype(BF16)
    xc_b = xc.astype(BF16)
    x_b = x.astype(BF16)
    nblk = wq_ref.shape[0]
    bw = wq_ref.shape[1]
    gates = jnp.zeros((tf, LANES), F32) + bg_ref[...]
    for part, (src, w_ref, o_ref, scl) in enumerate(
            ((xc_b, wq_ref, q_ref, 1.0), (xc_b, wk_ref, k_ref, k_scale), (x_b, wv_ref, v_ref, 1.0))):
        for jb in range(nblk):
            sl = slice(jb * bw, (jb + 1) * bw)
            y = _dot(src[:, sl], w_ref[jb])
            yb = y.astype(BF16)
            o_ref[:, sl] = yb if scl == 1.0 else (y * scl).astype(BF16)
            gates = gates + _dot(yb, wg_ref[part * inner + jb * bw:part * inner + (jb + 1) * bw, :])
    lane = lax.broadcasted_iota(jnp.int32, gates.shape, 1)
    is_forget = (lane % (2 * MLSTM_HEADS)) >= MLSTM_HEADS
    gates = jnp.where(is_forget, _log_sigmoid(gates), gates)
    g1 = pltpu.roll(gates, LANES - 2 * MLSTM_HEADS, axis=1)
    g_ref[0] = gates
    g_ref[1] = g1
    gt_ref[0] = gates.T[0:2 * SUBLANES, :]
    gt_ref[1] = g1.T[0:2 * SUBLANES, :]


def _mlstm_features(rows_feat, u, w_up, conv_w, conv_b, wq, wk, wv, wg, bg, *, k_scale):
    r, d = u.shape
    inner = w_up.shape[1] // 2
    tf = rows_feat.tile
    halo = HALO
    nh = r // halo
    u_main = pl.BlockSpec((tf, d), lambda i: (i, 0))
    u_prev = pl.BlockSpec((halo, d), lambda i: (jnp.maximum(i * (tf // halo) - 1, 0), 0))
    u_next = pl.BlockSpec((halo, d), lambda i: (jnp.minimum((i + 1) * (tf // halo), nh - 1), 0))
    main = pl.BlockSpec((tf, inner), lambda i: (i, 0))
    bf_out = jax.ShapeDtypeStruct((r, inner), BF16)
    return pl.pallas_call(
        functools.partial(_feat_kernel, tf=tf, lat_tiles=rows_feat.lat_tiles,
                          lat_tps=rows_feat.tiles_per_seq, ctx_tps=rows_feat.ctx // tf,
                          k_scale=k_scale),
        grid=(rows_feat.tiles,),
        in_specs=[u_main, u_prev, u_next, _resident(w_up.shape),
                  _resident(conv_w.shape), _resident(conv_b.shape),
                  _resident(wq.shape), _resident(wk.shape), _resident(wv.shape),
                  _resident(wg.shape), _resident(bg.shape)],
        out_specs=[main, main, main, main, main,
                   pl.BlockSpec((2, tf, LANES), lambda i: (0, i, 0)),
                   pl.BlockSpec((2, 2 * SUBLANES, tf), lambda i: (0, 0, i))],
        out_shape=[bf_out, bf_out, bf_out, bf_out, bf_out,
                   jax.ShapeDtypeStruct((2, r, LANES), F32),
                   jax.ShapeDtypeStruct((2, 2 * SUBLANES, r), F32)],
        scratch_shapes=[pltpu.VMEM((tf + 2 * halo, inner), F32)],
        compiler_params=_cparams(1),
        name="mlstm_features",
    )(u, u, u, w_up, conv_w, conv_b, wq, wk, wv, wg, bg)


def _split3(x):
    a = x.astype(BF16)
    r = x - a.astype(F32)
    b = r.astype(BF16)
    c = (r - b.astype(F32)).astype(BF16)
    return a, b, c


def _scan_kernel(q_ref, k_ref, v_ref, g_ref, gt_ref, h_ref, c_scr, cb_scr, n_scr, m_scr,
                 *, chunk, dh):
    d = pl.program_id(1)
    p = pl.program_id(2)

    @pl.when(p == 0)
    def _():
        c_scr[...] = jnp.zeros(c_scr.shape, F32)
        cb_scr[...] = jnp.zeros(cb_scr.shape, BF16)
        n_scr[...] = jnp.zeros(n_scr.shape, F32)
        m_scr[...] = jnp.zeros(m_scr.shape, F32)

    g = g_ref[0]
    gt = gt_ref[0]
    ti = lax.broadcasted_iota(jnp.int32, (chunk, chunk), 0)
    si = lax.broadcasted_iota(jnp.int32, (chunk, chunk), 1)
    sgn = 1 - 2 * d
    vis = (ti - si) * sgn >= 0
    tri = jnp.where(vis, 1.0, 0.0).astype(BF16)
    tri_t = jnp.where((si - ti) * sgn >= 0, 1.0, 0.0).astype(BF16)
    bc_all = functools.reduce(lambda a, b: a + b, [_dot(tri, t) for t in _split3(g)])
    br_all = functools.reduce(lambda a, b: a + b, [_dot(t, tri_t) for t in _split3(gt)])
    neg_inf = jnp.float32(-jnp.inf)
    for hd in range(MLSTM_HEADS):
        sl = slice(hd * dh, (hd + 1) * dh)
        q = q_ref[:, sl]
        k = k_ref[:, sl]
        v = v_ref[:, sl]
        ic = g[:, hd:hd + 1]
        fc = g[:, MLSTM_HEADS + hd:MLSTM_HEADS + hd + 1]
        bc = bc_all[:, MLSTM_HEADS + hd:MLSTM_HEADS + hd + 1]
        ir = gt[hd:hd + 1, :]
        br = br_all[MLSTM_HEADS + hd:MLSTM_HEADS + hd + 1, :]
        m_prev = m_scr[hd]
        n_vec = n_scr[hd]
        log_d = jnp.where(vis, bc - br + ir, neg_inf)
        m_inter = bc + m_prev
        m_t = jnp.maximum(m_inter, jnp.max(log_d, axis=-1, keepdims=True))
        s = _dot_nt(q, k) * jnp.exp(log_d - m_t)
        w_inter = jnp.exp(m_inter - m_t)
        qf = q.astype(F32)
        num = _dot(s.astype(BF16), v) + w_inter * _dot_nt(q, cb_scr[hd])
        den = jnp.sum(s, axis=-1, keepdims=True) + w_inter * jnp.sum(qf * n_vec, axis=-1, keepdims=True)
        h_ref[0, :, sl] = (num / jnp.maximum(jnp.abs(den), jnp.exp(-m_t))).astype(h_ref.dtype)
        b_tot = jnp.sum(fc, axis=0, keepdims=True)
        log_w = b_tot - bc + ic
        m_new = jnp.maximum(b_tot + m_prev, jnp.max(log_w, axis=0, keepdims=True))
        w_s = jnp.exp(log_w - m_new)
        decay = jnp.exp(b_tot + m_prev - m_new)
        vw = (v.astype(F32) * w_s).astype(BF16)
        c_new = decay * c_scr[hd] + _dot_tn(vw, k)
        c_scr[hd] = c_new
        cb_scr[hd] = c_new.astype(BF16)
        n_scr[hd] = decay * n_vec + jnp.sum(k.astype(F32) * w_s, axis=0, keepdims=True)
        m_scr[hd] = m_new


def _mlstm_scan(q, k, v, g, gt, *, batch, seq, ctx, chunk):
    r, inner = q.shape
    dh = inner // MLSTM_HEADS
    ncc = ctx // chunk
    ncl = seq // chunk
    lat_blocks = batch * ncl

    def blk(b, d, p):
        jc = jnp.where(d == 0, p, ncc - 1 - p)
        jl = jnp.where(d == 0, p - ncc, ncl - 1 - (p - ncc))
        return jnp.where(p < ncc, lat_blocks + b * ncc + jc, b * ncl + jl)

    row_spec = pl.BlockSpec((chunk, inner), lambda b, d, p: (blk(b, d, p), 0))
    return pl.pallas_call(
        functools.partial(_scan_kernel, chunk=chunk, dh=dh),
        grid=(batch, 2, ncc + ncl),
        in_specs=[row_spec, row_spec, row_spec,
                  pl.BlockSpec((1, chunk, LANES), lambda b, d, p: (d, blk(b, d, p), 0)),
                  pl.BlockSpec((1, 2 * SUBLANES, chunk), lambda b, d, p: (d, 0, blk(b, d, p)))],
        out_specs=pl.BlockSpec((1, chunk, inner), lambda b, d, p: (d, blk(b, d, p), 0)),
        out_shape=jax.ShapeDtypeStruct((2, r, inner), BF16),
        scratch_shapes=[pltpu.VMEM((MLSTM_HEADS, dh, dh), F32),
                        pltpu.VMEM((MLSTM_HEADS, dh, dh), BF16),
                        pltpu.VMEM((MLSTM_HEADS, 1, dh), F32),
                        pltpu.VMEM((MLSTM_HEADS, 1, 1), F32)],
        compiler_params=_cparams(3),
        name="mlstm_scan",
    )(q, k, v, g, gt)


def _mlstm_out_kernel(hd_ref, xc_ref, z_ref, skip_ref, nw_ref, a_ref, *, dh):
    for hd in range(MLSTM_HEADS):
        sl = slice(hd * dh, (hd + 1) * dh)
        hh = hd_ref[0, :, sl].astype(F32) + hd_ref[1, :, sl].astype(F32)
        mu = jnp.mean(hh, axis=-1, keepdims=True)
        cen = hh - mu
        var = jnp.mean(cen * cen, axis=-1, keepdims=True)
        hn = cen * lax.rsqrt(var + HEAD_LN_EPS) * nw_ref[:, sl]
        y = (hn + skip_ref[:, sl] * xc_ref[:, sl].astype(F32)) * _silu(z_ref[:, sl].astype(F32))
        a_ref[:, sl] = y.astype(BF16)


def _mlstm_out(rows, n_tiles, hdir, xc, z, skip, norm_w):
    inner = xc.shape[1]
    tm = rows.tile
    row_spec = pl.BlockSpec((tm, inner), lambda i: (i, 0))
    return pl.pallas_call(
        functools.partial(_mlstm_out_kernel, dh=inner // MLSTM_HEADS),
        grid=(n_tiles,),
        in_specs=[pl.BlockSpec((2, tm, inner), lambda i: (0, i, 0)), row_spec, row_spec,
                  _resident(skip.shape), _resident(norm_w.shape)],
        out_specs=row_spec,
        out_shape=jax.ShapeDtypeStruct((n_tiles * tm, inner), BF16),
        compiler_params=_cparams(1),
        name="mlstm_out",
    )(hdir, xc, z, skip, norm_w)


def _blockdiag_dense(w, block):
    g = w.shape[0]
    per = block // QKV_BLOCK
    rows = w.reshape(g // per, block, QKV_BLOCK)
    tiled = jnp.tile(rows, (1, 1, per))
    r = lax.broadcasted_iota(jnp.int32, (block, block), 0) // QKV_BLOCK
    c = lax.broadcasted_iota(jnp.int32, (block, block), 1) // QKV_BLOCK
    return jnp.where(r == c, tiled, jnp.zeros_like(tiled))


def _pick_tile(seq, ctx_rows, target):
    t = target
    while seq % t or ctx_rows % t:
        t //= 2
    return t


def kernel(x, c, ctx, c_ctx, w_mod, b_mod, norm_w, ffn_w_in, ffn_w_out,
           attn_w_qkv, attn_w_o, attn_lambda, attn_subln_w,
           mlstm_w_up, mlstm_conv_w, mlstm_conv_b, mlstm_w_qkv, mlstm_w_gates, mlstm_b_gates,
           mlstm_skip, mlstm_norm_w, mlstm_w_down, final_norm_w):
    batch, seq, d = x.shape
    ctx_len = ctx.shape[1]
    depth = w_mod.shape[0]
    inner = mlstm_w_down.shape[1]
    dh = inner // MLSTM_HEADS

    rows = _Rows(batch, seq, ctx_len, _pick_tile(seq, batch * ctx_len, 512))
    rows_feat = _Rows(batch, seq, ctx_len, _pick_tile(seq, ctx_len, 256))
    chunk = _pick_tile(seq, ctx_len, 256)
    tq = _pick_tile(seq, seq, 2048)

    n_mod_rows = -(-(batch + 1) // MOD_ROWS_PAD) * MOD_ROWS_PAD
    cc = jnp.concatenate([c, c_ctx[None, :], jnp.zeros((n_mod_rows - batch - 1, d), F32)], axis=0)
    mod_all = _modulation(cc, w_mod, b_mod).reshape(depth, n_mod_rows, N_MOD, d)

    cos, sin = _rope_tables(seq, rows.tile)
    fin_w = final_norm_w.reshape(1, d)
    h_srcs = (x.reshape(batch * seq, d), ctx.reshape(batch * ctx_len, d))
    w_in_bf = ffn_w_in.astype(BF16)
    w_out_bf = ffn_w_out.astype(BF16)

    for i in range(depth):
        last = i == depth - 1
        j = i // N_MIXERS
        mod = mod_all[i]
        nw = norm_w[i]
        h, u = _half_ffn(rows, rows.tiles, h_srcs, mod, nw, w_in_bf, w_out_bf, (i, 0),
                         fin_w, sub=0, emit="mixer_in")
        n_post = rows.lat_tiles if last else rows.tiles
        if i % N_MIXERS == 0:
            lambda_init = 0.8 - 0.6 * math.exp(-0.3 * i)
            q, k, v = _attn_qkv(rows, u, attn_w_qkv[j].astype(BF16), cos, sin)
            lam = attn_lambda[j]
            subln = attn_subln_w[j].reshape(1, ATTN_V_DIM)
            a_srcs = tuple(_attention(q, k, v, lam, subln, batch=batch, seq=seq, ctx=ctx_len, tq=tq,
                                      lambda_init=lambda_init))
            w_proj = attn_w_o[j].astype(BF16)
        else:
            wq = _blockdiag_dense(mlstm_w_qkv[j, 0], MXU_DIM).astype(BF16)
            wk = _blockdiag_dense(mlstm_w_qkv[j, 1], MXU_DIM).astype(BF16)
            wv = _blockdiag_dense(mlstm_w_qkv[j, 2], MXU_DIM).astype(BF16)
            n_gate = mlstm_w_gates.shape[2]
            wg = jnp.pad(mlstm_w_gates[j], ((0, 0), (0, LANES - n_gate))).astype(BF16)
            bg = jnp.pad(mlstm_b_gates[j], (0, LANES - n_gate)).reshape(1, LANES)
            q, k, v, xc, z, g, gt = _mlstm_features(
                rows_feat, u, mlstm_w_up[j].astype(BF16),
                mlstm_conv_w[j], mlstm_conv_b[j].reshape(1, inner),
                wq, wk, wv, wg, bg, k_scale=dh ** -0.5)
            hdir = _mlstm_scan(q, k, v, g, gt, batch=batch, seq=seq, ctx=ctx_len, chunk=chunk)
            a = _mlstm_out(rows, n_post, hdir, xc, z, mlstm_skip[j].reshape(1, inner),
                           mlstm_norm_w[j].reshape(1, inner))
            a_srcs = (a,)
            w_proj = mlstm_w_down[j].astype(BF16)
        (h,) = _half_ffn(rows, n_post, (h,), mod, nw, w_in_bf, w_out_bf, (i, 1),
                         fin_w, sub=2, emit="final" if last else "plain",
                         a_srcs=a_srcs, w_proj=w_proj)
        h_srcs = (h,)
    return h.reshape(batch, seq, d)
```
